```python
import math
import jax
import jax.numpy as jnp
from jax import lax
import numpy as np

D_MODEL = 1024
BATCH = 4
SEQ = 4096
DEPTH = 4
DEC_BATCH = 32
DEC_SEQ = 4
PAST_LEN = 8192
PAGE_SIZE = 128

N_META = 16
Q_BLOCK = 128
MIX_WIDTH = D_MODEL
SB_WIDTH = MIX_WIDTH // 4
SB_HEADS = 4
SB_HEAD_DIM = SB_WIDTH // SB_HEADS
SSM_WIDTH = MIX_WIDTH // 4
SSM_GROUP = 16
SSM_GROUPS = SSM_WIDTH // SSM_GROUP
SSM_STATE = 64
DIFF_WIDTH = MIX_WIDTH - SB_WIDTH - SSM_WIDTH
DIFF_HEADS = 4
DIFF_V_DIM = DIFF_WIDTH // DIFF_HEADS
DIFF_QK_DIM = DIFF_V_DIM // 2
ROPE_THETA = 10000.0
LN_EPS = 1e-5
NEG_INF = -1e30
SPLIT_SIZES = (SB_WIDTH,) * 4 + (SSM_WIDTH,) * 2 + (DIFF_WIDTH,) * 4
PROJ_WIDTH = sum(SPLIT_SIZES)

kernel_name = 'hybrid_stickbreak_s5_diffattn_step'

F32 = jnp.float32


def _layer_norm(x, g, b):
    xf = x.astype(F32)
    xc = xf - jnp.mean(xf, axis=-1, keepdims=True)
    var = jnp.mean(xc * xc, axis=-1, keepdims=True)
    return (xc * lax.rsqrt(var + LN_EPS) * g.astype(F32) + b.astype(F32)).astype(x.dtype)


def _rope(x, pos):
    d = x.shape[-1]
    half = d // 2
    inv = ROPE_THETA ** (-2.0 * jnp.arange(half, dtype=F32) / d)
    ang = pos.astype(F32)[:, None] * inv[None, :]
    shape = (1, pos.shape[0]) + (1,) * (x.ndim - 3) + (half,)
    cos = jnp.cos(ang).reshape(shape)
    sin = jnp.sin(ang).reshape(shape)
    xf = x.astype(F32)
    x1, x2 = xf[..., :half], xf[..., half:]
    return jnp.concatenate([x1 * cos - x2 * sin, x2 * cos + x1 * sin], axis=-1).astype(x.dtype)


def _stick_breaking_block(q, qpos, k, v, kpos):
    z = jnp.einsum('bqhd,bkhd->bhqk', q, k).astype(F32) * (q.shape[-1] ** -0.5)
    mask = kpos[None, :] < qpos[:, None]
    log_keep = jnp.where(mask, jax.nn.log_sigmoid(-z), 0.0)
    between = lax.cumsum(log_keep, axis=3, reverse=True) - log_keep
    w = jnp.where(mask, jnp.exp(jax.nn.log_sigmoid(z) + between), 0.0)
    return jnp.einsum('bhqk,bkhd->bqhd', w.astype(v.dtype), v)


def _diff_block(q, qpos, k, v, kpos, lam):
    s = jnp.einsum('bqhcd,bkhcd->bhcqk', q, k).astype(F32) * (q.shape[-1] ** -0.5)
    mask = kpos[None, :] <= qpos[:, None]
    p = jax.nn.softmax(jnp.where(mask, s, NEG_INF), axis=-1)
    w = p[:, :, 0] - lam * p[:, :, 1]
    return jnp.einsum('bhqk,bkhd->bqhd', w.astype(v.dtype), v)


def _sweep(block_fn, q, qpos, blocked):
    if not blocked:
        return block_fn(q, qpos)
    meta_out = block_fn(q[:, :N_META], qpos[:N_META])
    rest = q[:, N_META:]
    bsz = rest.shape[0]
    nb = rest.shape[1] // Q_BLOCK
    qb = jnp.moveaxis(rest.reshape((bsz, nb, Q_BLOCK) + rest.shape[2:]), 1, 0)
    pb = qpos[N_META:].reshape(nb, Q_BLOCK)
    out = lax.map(lambda a: block_fn(a[0], a[1]), (qb, pb))
    tail = out.shape[3:]
    out = jnp.moveaxis(out, 0, 1).reshape((bsz, nb * Q_BLOCK) + tail)
    return jnp.concatenate([meta_out, out], axis=1)


def _ssm_branch(u, lp, h0):
    bsz, t = u.shape[0], u.shape[1]
    uf = u.astype(F32).reshape(bsz, t, SSM_GROUPS, SSM_GROUP)
    lam_re = lp['ssm_a_re'].astype(F32)
    lam_im = lp['ssm_a_im'].astype(F32)
    dt = jnp.exp(lp['ssm_log_dt'].astype(F32))[:, None]
    mag = jnp.exp(lam_re * dt)
    ab_re = mag * jnp.cos(lam_im * dt)
    ab_im = mag * jnp.sin(lam_im * dt)
    den = lam_re * lam_re + lam_im * lam_im
    nr = ab_re - 1.0
    f_re = (nr * lam_re + ab_im * lam_im) / den
    f_im = (ab_im * lam_re - nr * lam_im) / den
    bu_re = jnp.einsum('btgp,gnp->btgn', uf, lp['ssm_b_re'].astype(F32))
    bu_im = jnp.einsum('btgp,gnp->btgn', uf, lp['ssm_b_im'].astype(F32))
    x_re = f_re * bu_re - f_im * bu_im
    x_im = f_re * bu_im + f_im * bu_re
    a_re = jnp.broadcast_to(ab_re, x_re.shape)
    a_im = jnp.broadcast_to(ab_im, x_re.shape)

    def combine(e1, e2):
        ar1, ai1, br1, bi1 = e1
        ar2, ai2, br2, bi2 = e2
        return (ar2 * ar1 - ai2 * ai1, ar2 * ai1 + ai2 * ar1,
                ar2 * br1 - ai2 * bi1 + br2, ar2 * bi1 + ai2 * br1 + bi2)

    p_re, p_im, h_re, h_im = lax.associative_scan(combine, (a_re, a_im, x_re, x_im), axis=1)
    if h0 is not None:
        h0r = h0[0].astype(F32)[:, None]
        h0i = h0[1].astype(F32)[:, None]
        h_re, h_im = h_re + p_re * h0r - p_im * h0i, h_im + p_re * h0i + p_im * h0r
    y = (jnp.einsum('btgn,gpn->btgp', h_re, lp['ssm_c_re'].astype(F32))
         - jnp.einsum('btgn,gpn->btgp', h_im, lp['ssm_c_im'].astype(F32))
         + lp['ssm_d'].astype(F32) * uf)
    y = jax.nn.gelu(y.reshape(bsz, t, SSM_WIDTH))
    gl = y @ lp['ssm_w_glu'].astype(F32)
    out = gl[..., :SSM_WIDTH] * jax.nn.sigmoid(gl[..., SSM_WIDTH:])
    return out.astype(u.dtype), h_re[:, -1], h_im[:, -1]


def _mixer(h, pos, lp, lam_init, past):
    bsz, t = h.shape[0], h.shape[1]
    proj = h @ lp['w_in']
    qa, ka, va, ga, ub, gb, qc, kc, vc, gc = jnp.split(
        proj, np.cumsum(SPLIT_SIZES)[:-1].tolist(), axis=-1)
    qa = qa.reshape(bsz, t, SB_HEADS, SB_HEAD_DIM)
    ka = ka.reshape(bsz, t, SB_HEADS, SB_HEAD_DIM)
    va = va.reshape(bsz, t, SB_HEADS, SB_HEAD_DIM)
    qc = _rope(qc.reshape(bsz, t, DIFF_HEADS, 2, DIFF_QK_DIM), pos)
    kc = _rope(kc.reshape(bsz, t, DIFF_HEADS, 2, DIFF_QK_DIM), pos)
    vc = vc.reshape(bsz, t, DIFF_HEADS, DIFF_V_DIM)
    if past is None:
        ka_all, va_all, kc_all, vc_all, kpos = ka, va, kc, vc, pos
        h0 = None
        blocked = True
    else:
        n_past = past['sb_k'].shape[1]
        ka_all = jnp.concatenate([past['sb_k'].astype(ka.dtype), ka], axis=1)
        va_all = jnp.concatenate([past['sb_v'].astype(va.dtype), va], axis=1)
        kc_all = jnp.concatenate([past['diff_k'].astype(kc.dtype).reshape(
            bsz, n_past, DIFF_HEADS, 2, DIFF_QK_DIM), kc], axis=1)
        vc_all = jnp.concatenate([past['diff_v'].astype(vc.dtype), vc], axis=1)
        kpos = jnp.concatenate([jnp.arange(n_past, dtype=jnp.int32), pos])
        h0 = (past['ssm_re'], past['ssm_im'])
        blocked = False
    sb = _sweep(lambda qb, pb: _stick_breaking_block(qb, pb, ka_all, va_all, kpos), qa, pos, blocked)
    lam = (jnp.exp(jnp.sum(lp['lq1'].astype(F32) * lp['lk1'].astype(F32)))
           - jnp.exp(jnp.sum(lp['lq2'].astype(F32) * lp['lk2'].astype(F32))) + lam_init)
    dif = _sweep(lambda qb, pb: _diff_block(qb, pb, kc_all, vc_all, kpos, lam), qc, pos, blocked)
    df = dif.astype(F32)
    df = (df * lax.rsqrt(jnp.mean(df * df, axis=-1, keepdims=True) + LN_EPS)
          * lp['subln'].astype(F32) * (1.0 - lam_init)).astype(h.dtype)
    ssm, hr, hi = _ssm_branch(ub, lp, h0)
    mixed = jnp.concatenate([
        sb.reshape(bsz, t, SB_WIDTH) * jax.nn.silu(ga),
        ssm * jax.nn.silu(gb),
        df.reshape(bsz, t, DIFF_WIDTH) * jax.nn.silu(gc)], axis=-1)
    out = mixed @ lp['w_out']
    rows = (ka, va, kc.reshape(bsz, t, 2 * DIFF_HEADS, DIFF_QK_DIM), vc, hr, hi)
    return out, rows


def _gather_pages(cache, layer, page_table):
    g = cache[layer, page_table]
    return g.reshape((g.shape[0], g.shape[1] * g.shape[2]) + g.shape[3:])


def setup_inputs(seed: int = 0) -> dict:
    key = jax.random.key(seed)
    ks = iter(jax.random.split(key, 40))

    def nrm(shape, scale):
        return scale * jax.random.normal(next(ks), shape, F32)

    n_pages = PAST_LEN // PAGE_SIZE
    n_used = DEC_BATCH * n_pages
    n_pool = n_used + max(1, n_used // 4)
    out_scale = (8 * DEPTH) ** -0.25
    inputs = {}
    inputs['x_prompt'] = nrm((BATCH, SEQ, D_MODEL), 1.0)
    inputs['x_sample'] = nrm((DEC_BATCH, DEC_SEQ, D_MODEL), 1.0)
    inputs['cache_sb_k'] = nrm((DEPTH, n_pool, PAGE_SIZE, SB_HEADS, SB_HEAD_DIM), 1.0)
    inputs['cache_sb_v'] = nrm((DEPTH, n_pool, PAGE_SIZE, SB_HEADS, SB_HEAD_DIM), 1.0)
    inputs['cache_diff_k'] = nrm((DEPTH, n_pool, PAGE_SIZE, 2 * DIFF_HEADS, DIFF_QK_DIM), 1.0)
    inputs['cache_diff_v'] = nrm((DEPTH, n_pool, PAGE_SIZE, DIFF_HEADS, DIFF_V_DIM), 1.0)
    inputs['state_ssm_re'] = nrm((DEPTH, DEC_BATCH, SSM_GROUPS, SSM_STATE), 0.3)
    inputs['state_ssm_im'] = nrm((DEPTH, DEC_BATCH, SSM_GROUPS, SSM_STATE), 0.3)
    inputs['page_table'] = jax.random.permutation(next(ks), n_pool)[:n_used].reshape(
        DEC_BATCH, n_pages).astype(jnp.int32)
    inputs['meta_tokens'] = nrm((N_META, D_MODEL), 1.0)
    inputs['ln_in_g'] = 1.0 + nrm((D_MODEL,), 0.02)
    inputs['ln_in_b'] = nrm((D_MODEL,), 0.02)
    inputs['w_in'] = nrm((DEPTH, D_MODEL, PROJ_WIDTH), D_MODEL ** -0.5)
    inputs['w_out'] = nrm((DEPTH, MIX_WIDTH, D_MODEL), out_scale * MIX_WIDTH ** -0.5)
    inputs['ln_g'] = 1.0 + nrm((DEPTH, D_MODEL), 0.02)
    inputs['ln_b'] = nrm((DEPTH, D_MODEL), 0.02)
    inputs['ssm_a_re'] = -0.5 + nrm((DEPTH, SSM_GROUPS, SSM_STATE), 0.01)
    inputs['ssm_a_im'] = jnp.pi * jnp.arange(SSM_STATE, dtype=F32) + nrm((DEPTH, SSM_GROUPS, SSM_STATE), 0.01)
    inputs['ssm_log_dt'] = jax.random.uniform(next(ks), (DEPTH, SSM_GROUPS), F32,
                                              math.log(1e-3), math.log(1e-1))
    inputs['ssm_b_re'] = nrm((DEPTH, SSM_GROUPS, SSM_STATE, SSM_GROUP), (2 * SSM_GROUP) ** -0.5)
    inputs['ssm_b_im'] = nrm((DEPTH, SSM_GROUPS, SSM_STATE, SSM_GROUP), (2 * SSM_GROUP) ** -0.5)
    inputs['ssm_c_re'] = nrm((DEPTH, SSM_GROUPS, SSM_GROUP, SSM_STATE), SSM_STATE ** -0.5)
    inputs['ssm_c_im'] = nrm((DEPTH, SSM_GROUPS, SSM_GROUP, SSM_STATE), SSM_STATE ** -0.5)
    inputs['ssm_d'] = nrm((DEPTH, SSM_GROUPS, SSM_GROUP), 1.0)
    inputs['ssm_w_glu'] = nrm((DEPTH, SSM_WIDTH, 2 * SSM_WIDTH), SSM_WIDTH ** -0.5)
    inputs['diff_lq1'] = nrm((DEPTH, DIFF_QK_DIM), 0.1)
    inputs['diff_lk1'] = nrm((DEPTH, DIFF_QK_DIM), 0.1)
    inputs['diff_lq2'] = nrm((DEPTH, DIFF_QK_DIM), 0.1)
    inputs['diff_lk2'] = nrm((DEPTH, DIFF_QK_DIM), 0.1)
    inputs['diff_subln'] = 1.0 + nrm((DEPTH, DIFF_V_DIM), 0.02)
    return inputs


def reference(x_prompt, x_sample, cache_sb_k, cache_sb_v, cache_diff_k, cache_diff_v,
              state_ssm_re, state_ssm_im, page_table, meta_tokens, ln_in_g, ln_in_b,
              w_in, w_out, ln_g, ln_b, ssm_a_re, ssm_a_im, ssm_log_dt, ssm_b_re, ssm_b_im,
              ssm_c_re, ssm_c_im, ssm_d, ssm_w_glu, diff_lq1, diff_lk1, diff_lq2, diff_lk2,
              diff_subln):
    alpha = (2 * DEPTH) ** 0.25
    bsz = x_prompt.shape[0]
    meta = jnp.broadcast_to(meta_tokens.astype(x_prompt.dtype)[None], (bsz, N_META, D_MODEL))
    hp = _layer_norm(jnp.concatenate([meta, x_prompt], axis=1), ln_in_g, ln_in_b)
    hs = _layer_norm(x_sample, ln_in_g, ln_in_b)
    pos_p = jnp.arange(hp.shape[1], dtype=jnp.int32)
    pos_s = PAST_LEN + jnp.arange(hs.shape[1], dtype=jnp.int32)
    new_p = [[] for _ in range(6)]
    new_s = [[] for _ in range(6)]
    for l in range(DEPTH):
        lp = {'w_in': w_in[l], 'w_out': w_out[l],
              'ssm_a_re': ssm_a_re[l], 'ssm_a_im': ssm_a_im[l], 'ssm_log_dt': ssm_log_dt[l],
              'ssm_b_re': ssm_b_re[l], 'ssm_b_im': ssm_b_im[l],
              'ssm_c_re': ssm_c_re[l], 'ssm_c_im': ssm_c_im[l],
              'ssm_d': ssm_d[l], 'ssm_w_glu': ssm_w_glu[l],
              'lq1': diff_lq1[l], 'lk1': diff_lk1[l], 'lq2': diff_lq2[l], 'lk2': diff_lk2[l],
              'subln': diff_subln[l]}
        lam_init = 0.8 - 0.6 * math.exp(-0.3 * l)
        out_p, rows_p = _mixer(hp, pos_p, lp, lam_init, None)
        hp = _layer_norm(alpha * hp + out_p, ln_g[l], ln_b[l])
        past = {'sb_k': _gather_pages(cache_sb_k, l, page_table),
                'sb_v': _gather_pages(cache_sb_v, l, page_table),
                'diff_k': _gather_pages(cache_diff_k, l, page_table),
                'diff_v': _gather_pages(cache_diff_v, l, page_table),
                'ssm_re': state_ssm_re[l], 'ssm_im': state_ssm_im[l]}
        out_s, rows_s = _mixer(hs, pos_s, lp, lam_init, past)
        hs = _layer_norm(alpha * hs + out_s, ln_g[l], ln_b[l])
        for lst, r in zip(new_p, rows_p):
            lst.append(r)
        for lst, r in zip(new_s, rows_s):
            lst.append(r)
    sb_k_p, sb_v_p, diff_k_p, diff_v_p, ssm_re_p, ssm_im_p = [jnp.stack(a) for a in new_p]
    sb_k_s, sb_v_s, diff_k_s, diff_v_s, ssm_re_s, ssm_im_s = [jnp.stack(a) for a in new_s]
    y_prompt = hp[:, N_META:]
    y_sample = hs
    return (y_prompt, y_sample, sb_k_p, sb_v_p, diff_k_p, diff_v_p, ssm_re_p, ssm_im_p,
            sb_k_s, sb_v_s, diff_k_s, diff_v_s, ssm_re_s, ssm_im_s)
```

```python
import functools
import math

import jax
import jax.numpy as jnp
from jax import lax
from jax.experimental import pallas as pl
from jax.experimental.pallas import tpu as pltpu

F32 = jnp.float32
BF16 = jnp.bfloat16

N_META = 16
LN_EPS = 1e-5
ROPE_THETA = 10000.0
NEG_INF = -1e30

SB_HEADS = 4
SB_HEAD_DIM = 64
SB_WIDTH = SB_HEADS * SB_HEAD_DIM
SSM_GROUPS = 16
SSM_GROUP = 16
SSM_STATE = 64
SSM_WIDTH = SSM_GROUPS * SSM_GROUP
SSM_LANES = SSM_GROUPS * SSM_STATE
DIFF_HEADS = 4
DIFF_QK_DIM = 64
DIFF_V_DIM = 128
DIFF_WIDTH = DIFF_HEADS * DIFF_V_DIM
QK_SCALE = 0.125

LANES = 128
SUBLANES = 8
TB = 128
ROW_TILE = 512
VMEM_LIMIT = 56 * 1024 * 1024

_NT = (((1,), (1,)), ((), ()))


def _params(sem):
    return pltpu.CompilerParams(dimension_semantics=sem, vmem_limit_bytes=VMEM_LIMIT)


def _dot(a, b):
    return jnp.dot(a, b, preferred_element_type=F32)


def _dot_nt(a, b):
    return lax.dot_general(a, b, _NT, preferred_element_type=F32)


def _sigmoid(x):
    return 1.0 / (1.0 + jnp.exp(-x))


def _layer_norm(x, g, b):
    xc = x - jnp.mean(x, axis=-1, keepdims=True)
    var = jnp.mean(xc * xc, axis=-1, keepdims=True)
    return xc * lax.rsqrt(var + LN_EPS) * g + b


def _row_tile(rows):
    for t in (ROW_TILE, 384, 256, LANES):
        if rows % t == 0:
            return t
    raise ValueError(f"row count {rows} is not a multiple of {LANES}")


def _ln_kernel(x_ref, g_ref, b_ref, o_ref):
    o_ref[...] = _layer_norm(x_ref[...], g_ref[...], b_ref[...])


def _ln_rows(x, g, b):
    rows, d = x.shape
    tm = _row_tile(rows)
    return pl.pallas_call(
        _ln_kernel,
        grid=(rows // tm,),
        in_specs=[pl.BlockSpec((tm, d), lambda i: (i, 0)),
                  pl.BlockSpec((1, d), lambda i: (0, 0)),
                  pl.BlockSpec((1, d), lambda i: (0, 0))],
        out_specs=pl.BlockSpec((tm, d), lambda i: (i, 0)),
        out_shape=jax.ShapeDtypeStruct((rows, d), F32),
        compiler_params=_params(("parallel",)),
        name="ln_in",
    )(x, g.reshape(1, d), b.reshape(1, d))


def _rope(x, cos, sin_signed):
    lane = lax.broadcasted_iota(jnp.int32, (x.shape[0], LANES), 1)
    first_half = (lane % DIFF_QK_DIM) < (DIFF_QK_DIM // 2)
    outs = []
    for c in range(x.shape[1] // LANES):
        xc = x[:, c * LANES:(c + 1) * LANES]
        partner = jnp.where(first_half,
                            pltpu.roll(xc, LANES - DIFF_QK_DIM // 2, axis=1),
                            pltpu.roll(xc, DIFF_QK_DIM // 2, axis=1))
        outs.append(xc * cos + partner * sin_signed)
    return jnp.concatenate(outs, axis=1)


_C_QA, _C_KA, _C_VA, _C_GA = 0, 256, 512, 768
_C_UB, _C_GB = 1024, 1280
_C_QC, _C_KC, _C_VC, _C_GC, _C_END = 1536, 2048, 2560, 3072, 3584


def _proj_kernel(h_ref, w_ref, cos_ref, sin_ref,
                 qa_ref, ka_ref, va_ref, kab_ref, vab_ref, sg_ref, ub_ref,
                 qc_ref, kc_ref, vc_ref, kcb_ref, vcb_ref):
    hb = h_ref[...].astype(BF16)

    def mm(c0, c1):
        return _dot(hb, w_ref[:, c0:c1])

    def silu(g):
        return g * _sigmoid(g)

    cos = cos_ref[...]
    sin = sin_ref[...]
    qa_ref[...] = (mm(_C_QA, _C_KA) * QK_SCALE).astype(BF16)
    ka = mm(_C_KA, _C_VA)
    ka_ref[...] = ka
    kab_ref[...] = ka.astype(BF16)
    va = mm(_C_VA, _C_GA)
    va_ref[...] = va
    vab_ref[...] = va.astype(BF16)
    sg_ref[:, 0:SB_WIDTH] = silu(mm(_C_GA, _C_UB))
    ub_ref[...] = mm(_C_UB, _C_GB)
    sg_ref[:, SB_WIDTH:SB_WIDTH + SSM_WIDTH] = silu(mm(_C_GB, _C_QC))
    qc_ref[...] = (_rope(mm(_C_QC, _C_KC), cos, sin) * QK_SCALE).astype(BF16)
    kc = _rope(mm(_C_KC, _C_VC), cos, sin)
    kc_ref[...] = kc
    kcb_ref[...] = kc.astype(BF16)
    vc = mm(_C_VC, _C_GC)
    vc_ref[...] = vc
    vcb_ref[...] = vc.astype(BF16)
    sg_ref[:, SB_WIDTH + SSM_WIDTH:] = silu(mm(_C_GC, _C_END))


def _proj(h, w_bf, cos, sin):
    rows, d = h.shape
    tm = _row_tile(rows)
    widths = [(SB_WIDTH, BF16), (SB_WIDTH, F32), (SB_WIDTH, F32), (SB_WIDTH, BF16), (SB_WIDTH, BF16),
              (d, F32), (SSM_WIDTH, F32),
              (DIFF_WIDTH, BF16), (DIFF_WIDTH, F32), (DIFF_WIDTH, F32), (DIFF_WIDTH, BF16), (DIFF_WIDTH, BF16)]
    return pl.pallas_call(
        _proj_kernel,
        grid=(rows // tm,),
        in_specs=[pl.BlockSpec((tm, d), lambda i: (i, 0)),
                  pl.BlockSpec(w_bf.shape, lambda i: (0, 0)),
                  pl.BlockSpec((tm, LANES), lambda i: (i, 0)),
                  pl.BlockSpec((tm, LANES), lambda i: (i, 0))],
        out_specs=[pl.BlockSpec((tm, w), lambda i: (i, 0)) for w, _ in widths],
        out_shape=[jax.ShapeDtypeStruct((rows, w), dt) for w, dt in widths],
        compiler_params=_params(("parallel",)),
        name="proj",
    )(h, w_bf, cos, sin)


def _out_kernel(sb_ref, ssm_ref, df_ref, sg_ref, h_ref, w_ref, g_ref, b_ref, o_ref, *, alpha):
    mixed = jnp.concatenate([sb_ref[...], ssm_ref[...], df_ref[...]], axis=1) * sg_ref[...]
    out = _dot(mixed.astype(BF16), w_ref[...])
    o_ref[...] = _layer_norm(alpha * h_ref[...] + out, g_ref[...], b_ref[...])


def _out_ln(sb, ssm, df, sg, h, w_bf, g, b, alpha):
    rows, d = h.shape
    tm = _row_tile(rows)

    def rowspec(w):
        return pl.BlockSpec((tm, w), lambda i: (i, 0))

    return pl.pallas_call(
        functools.partial(_out_kernel, alpha=alpha),
        grid=(rows // tm,),
        in_specs=[rowspec(SB_WIDTH), rowspec(SSM_WIDTH), rowspec(DIFF_WIDTH), rowspec(d), rowspec(d),
                  pl.BlockSpec(w_bf.shape, lambda i: (0, 0)),
                  pl.BlockSpec((1, d), lambda i: (0, 0)),
                  pl.BlockSpec((1, d), lambda i: (0, 0))],
        out_specs=rowspec(d),
        out_shape=jax.ShapeDtypeStruct((rows, d), F32),
        compiler_params=_params(("parallel",)),
        name="out_ln",
    )(sb, ssm, df, sg, h, w_bf, g.reshape(1, d), b.reshape(1, d))


def _sb_block(qm, kb, vbd, u2, carry, mask):
    ws, new_carry = [], []
    for h in range(SB_HEADS):
        z = _dot_nt(qm[h], kb)
        log_keep = -(jnp.maximum(z, 0.0) + jnp.log(1.0 + jnp.exp(-jnp.abs(z))))
        if mask is not None:
            log_keep = jnp.where(mask, log_keep, 0.0)
        hi = log_keep.astype(BF16)
        lo = (log_keep - hi.astype(F32)).astype(BF16)
        sums = _dot(jnp.concatenate([hi, lo], axis=1), u2)
        between = sums[:, :TB] + carry[h]
        w = jnp.exp(z + log_keep + between)
        if mask is not None:
            w = jnp.where(mask, w, 0.0)
        ws.append(w.astype(BF16))
        new_carry.append(carry[h] + sums[:, TB:])
    return _dot(jnp.concatenate(ws, axis=1), vbd), new_carry


def _head_masked(x, heads, width):
    col = lax.broadcasted_iota(jnp.int32, x.shape, 1)
    return [jnp.where((col >= h * width) & (col < (h + 1) * width), x, 0.0).astype(BF16)
            for h in range(heads)]


def _sb_prompt_kernel(q_ref, k_ref, v_ref, u2_ref, o_ref, vbd_ref, acc_ref, carry_ref, *, pad, nblk):
    i = pl.program_id(1)

    @pl.when(i == 0)
    def _build_values():
        def build(j, c):
            vb = v_ref[0, pl.ds(pl.multiple_of(j * TB, TB), TB), :].astype(F32)
            for h, vm in enumerate(_head_masked(vb, SB_HEADS, SB_HEAD_DIM)):
                vbd_ref[j, h * TB:(h + 1) * TB, :] = vm
            return c
        lax.fori_loop(0, nblk, build, 0)

    qm = _head_masked(q_ref[0].astype(F32), SB_HEADS, SB_HEAD_DIM)
    row = lax.broadcasted_iota(jnp.int32, (TB, TB), 0)
    col = lax.broadcasted_iota(jnp.int32, (TB, TB), 1)
    u2 = u2_ref[...]

    def step(j, mask, first):
        kb = k_ref[0, pl.ds(pl.multiple_of(j * TB, TB), TB), :]
        carry = [jnp.zeros((TB, TB), F32)] * SB_HEADS if first else [carry_ref[h] for h in range(SB_HEADS)]
        out, carry = _sb_block(qm, kb, vbd_ref[j], u2, carry, mask)
        acc_ref[...] = out if first else acc_ref[...] + out
        for h in range(SB_HEADS):
            carry_ref[h] = carry[h]

    step(i, (col < row) & (i * TB + col >= pad), True)

    def body(t, c):
        step(i - 1 - t, None, False)
        return c
    lax.fori_loop(0, jnp.maximum(i - 1, 0), body, 0)

    @pl.when(i > 0)
    def _first_block():
        step(0, col >= pad, False)

    o_ref[0] = acc_ref[...]


def _sb_prompt(q, k, v, u2, pad):
    bsz, tp, w = q.shape
    nblk = tp // TB
    return pl.pallas_call(
        functools.partial(_sb_prompt_kernel, pad=pad, nblk=nblk),
        grid=(bsz, nblk),
        in_specs=[pl.BlockSpec((1, TB, w), lambda b, i: (b, i, 0)),
                  pl.BlockSpec((1, tp, w), lambda b, i: (b, 0, 0)),
                  pl.BlockSpec((1, tp, w), lambda b, i: (b, 0, 0)),
                  pl.BlockSpec(u2.shape, lambda b, i: (0, 0))],
        out_specs=pl.BlockSpec((1, TB, w), lambda b, i: (b, i, 0)),
        out_shape=jax.ShapeDtypeStruct((bsz, tp, w), F32),
        scratch_shapes=[pltpu.VMEM((nblk, SB_HEADS * TB, w), BF16),
                        pltpu.VMEM((TB, w), F32),
                        pltpu.VMEM((SB_HEADS, TB, TB), F32)],
        compiler_params=_params(("parallel", "arbitrary")),
        name="sb_prompt",
    )(q, k, v, u2)


def _diff_lambda(lqk_ref, lam_init):
    lqk = lqk_ref[...]
    s1 = jnp.sum(lqk[0:1] * lqk[1:2], axis=1, keepdims=True)
    s2 = jnp.sum(lqk[2:3] * lqk[3:4], axis=1, keepdims=True)
    return jnp.exp(s1) - jnp.exp(s2) + lam_init


def _softmax_step(s, vb, m_prev, l_prev, acc_prev):
    m_new = jnp.maximum(m_prev, jnp.max(s, axis=1, keepdims=True))
    alpha = jnp.exp(m_prev - m_new)
    p = jnp.exp(s - m_new)
    l_new = alpha * l_prev + jnp.sum(p, axis=1, keepdims=True)
    acc_new = alpha * acc_prev + _dot(p.astype(BF16), vb)
    return m_new, l_new, acc_new


def _diff_combine(acc0, l0, acc1, l1, lam, subln, lam_init):
    d = acc0 / l0 - lam * (acc1 / l1)
    return d * lax.rsqrt(jnp.mean(d * d, axis=1, keepdims=True) + LN_EPS) * subln * (1.0 - lam_init)


def _diff_prompt_kernel(q_ref, k_ref, v_ref, lqk_ref, subln_ref, o_ref,
                        qs_ref, m_ref, l_ref, acc_ref, *, pad, lam_init):
    i = pl.program_id(1)
    half = 2 * DIFF_QK_DIM * 2
    qf = q_ref[0].astype(F32)
    for hh in range(DIFF_HEADS // 2):
        maps = _head_masked(qf[:, hh * half:(hh + 1) * half], 4, DIFF_QK_DIM)
        for hl in range(2):
            h = hh * 2 + hl
            qs_ref[h, 0:TB, :] = maps[hl * 2]
            qs_ref[h, TB:2 * TB, :] = maps[hl * 2 + 1]
    row = lax.broadcasted_iota(jnp.int32, (2 * TB, TB), 0) % TB
    col = lax.broadcasted_iota(jnp.int32, (2 * TB, TB), 1)

    def step(j, mask, first):
        r0 = pl.multiple_of(j * TB, TB)
        for h in range(DIFF_HEADS):
            hh = h // 2
            kb = k_ref[0, pl.ds(r0, TB), hh * half:(hh + 1) * half]
            vb = v_ref[0, pl.ds(r0, TB), h * DIFF_V_DIM:(h + 1) * DIFF_V_DIM]
            s = _dot_nt(qs_ref[h], kb)
            if mask is not None:
                s = jnp.where(mask, s, NEG_INF)
            if first:
                m_prev = jnp.full((2 * TB, 1), NEG_INF, F32)
                l_prev = jnp.zeros((2 * TB, 1), F32)
                acc_prev = jnp.zeros((2 * TB, DIFF_V_DIM), F32)
            else:
                m_prev, l_prev, acc_prev = m_ref[h], l_ref[h], acc_ref[h]
            m_ref[h], l_ref[h], acc_ref[h] = _softmax_step(s, vb, m_prev, l_prev, acc_prev)

    step(i, (col <= row) & (i * TB + col >= pad), True)

    def body(t, c):
        step(i - 1 - t, None, False)
        return c
    lax.fori_loop(0, jnp.maximum(i - 1, 0), body, 0)

    @pl.when(i > 0)
    def _first_block():
        step(0, col >= pad, False)

    lam = _diff_lambda(lqk_ref, lam_init)
    for h in range(DIFF_HEADS):
        acc, l = acc_ref[h], l_ref[h]
        o_ref[0, :, h * DIFF_V_DIM:(h + 1) * DIFF_V_DIM] = _diff_combine(
            acc[:TB], l[:TB], acc[TB:], l[TB:], lam, subln_ref[...], lam_init)


def _diff_prompt(q, k, v, lqk, subln, pad, lam_init):
    bsz, tp, w = q.shape
    nblk = tp // TB
    return pl.pallas_call(
        functools.partial(_diff_prompt_kernel, pad=pad, lam_init=lam_init),
        grid=(bsz, nblk),
        in_specs=[pl.BlockSpec((1, TB, w), lambda b, i: (b, i, 0)),
                  pl.BlockSpec((1, tp, w), lambda b, i: (b, 0, 0)),
                  pl.BlockSpec((1, tp, w), lambda b, i: (b, 0, 0)),
                  pl.BlockSpec(lqk.shape, lambda b, i: (0, 0)),
                  pl.BlockSpec(subln.shape, lambda b, i: (0, 0))],
        out_specs=pl.BlockSpec((1, TB, w), lambda b, i: (b, i, 0)),
        out_shape=jax.ShapeDtypeStruct((bsz, tp, w), F32),
        scratch_shapes=[pltpu.VMEM((DIFF_HEADS, 2 * TB, 2 * DIFF_V_DIM), BF16),
                        pltpu.VMEM((DIFF_HEADS, 2 * TB, 1), F32),
                        pltpu.VMEM((DIFF_HEADS, 2 * TB, 1), F32),
                        pltpu.VMEM((DIFF_HEADS, 2 * TB, DIFF_V_DIM), F32)],
        compiler_params=_params(("parallel", "arbitrary")),
        name="diff_prompt",
    )(q, k, v, lqk, subln)


def _ssm_param_kernel(are_ref, aim_ref, ldt_ref, o_ref):
    lam_re, lam_im = are_ref[...], aim_ref[...]
    dt = jnp.exp(ldt_ref[...])
    mag = jnp.exp(lam_re * dt)
    ab_re = mag * jnp.cos(lam_im * dt)
    ab_im = mag * jnp.sin(lam_im * dt)
    den = lam_re * lam_re + lam_im * lam_im
    nr = ab_re - 1.0
    o_ref[0] = ab_re
    o_ref[1] = ab_im
    o_ref[2] = (nr * lam_re + ab_im * lam_im) / den
    o_ref[3] = (ab_im * lam_re - nr * lam_im) / den


def _ssm_params(a_re, a_im, log_dt):
    out = pl.pallas_call(
        _ssm_param_kernel,
        out_shape=jax.ShapeDtypeStruct((4, SSM_GROUPS, SSM_STATE), F32),
        name="ssm_params",
    )(a_re, a_im, log_dt.reshape(SSM_GROUPS, 1))
    return out.reshape(4, SSM_LANES)


def _cmul(ar, ai, br, bi):
    return ar * br - ai * bi, ar * bi + ai * br


def _ssm_inputs(u, par_ref, bre_ref, bim_ref):
    ub = u.astype(BF16)
    return _cmul(par_ref[2:3, :], par_ref[3:4, :], _dot(ub, bre_ref[...]), _dot(ub, bim_ref[...]))


def _ssm_outputs(u, h_re, h_im, cre_ref, cim_ref, d_ref, wglu_ref):
    y = _dot(h_re.astype(BF16), cre_ref[...]) - _dot(h_im.astype(BF16), cim_ref[...]) + d_ref[...] * u
    y = 0.5 * y * (1.0 + jnp.tanh(math.sqrt(2.0 / math.pi) * (y + 0.044715 * (y * y * y))))
    gl = _dot(y.astype(BF16), wglu_ref[...])
    return gl[:, :SSM_WIDTH] * _sigmoid(gl[:, SSM_WIDTH:])


def _ssm_prompt_kernel(u_ref, par_ref, bre_ref, bim_ref, cre_ref, cim_ref, d_ref, wglu_ref,
                       o_ref, hre_ref, him_ref,
                       xr_ref, xi_ref, hr_ref, hi_ref, cst_ref, st_ref, *, pad, tt):
    t = pl.program_id(1)
    sub = lax.broadcasted_iota(jnp.int32, (SUBLANES, SSM_LANES), 0)

    @pl.when(t == 0)
    def _init():
        st_ref[...] = jnp.zeros_like(st_ref)
        pr, pi = [par_ref[0:1, :]], [par_ref[1:2, :]]
        for _ in range(SUBLANES - 1):
            nr, ni = _cmul(pr[-1], pi[-1], pr[0], pi[0])
            pr.append(nr)
            pi.append(ni)
        for n, k in enumerate((1, 2, 4)):
            cst_ref[2 * n] = jnp.where(sub >= k, pr[k - 1], 0.0)
            cst_ref[2 * n + 1] = jnp.where(sub >= k, pi[k - 1], 0.0)
        car_r = jnp.zeros((SUBLANES, SSM_LANES), F32)
        car_i = jnp.zeros((SUBLANES, SSM_LANES), F32)
        for s in range(SUBLANES):
            car_r = jnp.where(sub == s, pr[s], car_r)
            car_i = jnp.where(sub == s, pi[s], car_i)
        cst_ref[6] = car_r
        cst_ref[7] = car_i

    u = u_ref[0]
    x_re, x_im = _ssm_inputs(u, par_ref, bre_ref, bim_ref)
    live = (t * tt + lax.broadcasted_iota(jnp.int32, (tt, 1), 0)) >= pad
    xr_ref[...] = jnp.where(live, x_re, 0.0)
    xi_ref[...] = jnp.where(live, x_im, 0.0)

    def tile(n, carry):
        cr, ci = carry
        r0 = pl.multiple_of(n * SUBLANES, SUBLANES)
        hr = xr_ref[pl.ds(r0, SUBLANES), :]
        hi = xi_ref[pl.ds(r0, SUBLANES), :]
        for m, k in enumerate((1, 2, 4)):
            dr, di = _cmul(cst_ref[2 * m], cst_ref[2 * m + 1],
                           pltpu.roll(hr, k, axis=0), pltpu.roll(hi, k, axis=0))
            hr, hi = hr + dr, hi + di
        dr, di = _cmul(cst_ref[6], cst_ref[7], cr, ci)
        hr, hi = hr + dr, hi + di
        hr_ref[pl.ds(r0, SUBLANES), :] = hr
        hi_ref[pl.ds(r0, SUBLANES), :] = hi
        last = SUBLANES - 1
        return (jnp.broadcast_to(hr[last:last + 1, :], hr.shape),
                jnp.broadcast_to(hi[last:last + 1, :], hi.shape))

    cr, ci = lax.fori_loop(0, tt // SUBLANES, tile, (st_ref[0], st_ref[1]))
    st_ref[0] = cr
    st_ref[1] = ci
    hre_ref[0] = cr[0:1, :]
    him_ref[0] = ci[0:1, :]
    o_ref[0] = _ssm_outputs(u, hr_ref[...], hi_ref[...], cre_ref, cim_ref, d_ref, wglu_ref)


def _ssm_prompt(u, par, mats, pad):
    bsz, tp, w = u.shape
    tt = 384 if tp % 384 == 0 else TB

    def const(a):
        return pl.BlockSpec(a.shape, lambda b, t: (0,) * a.ndim)

    state = jax.ShapeDtypeStruct((bsz, 1, SSM_LANES), F32)
    return pl.pallas_call(
        functools.partial(_ssm_prompt_kernel, pad=pad, tt=tt),
        grid=(bsz, tp // tt),
        in_specs=[pl.BlockSpec((1, tt, w), lambda b, t: (b, t, 0)), const(par)] + [const(m) for m in mats],
        out_specs=[pl.BlockSpec((1, tt, w), lambda b, t: (b, t, 0)),
                   pl.BlockSpec((1, 1, SSM_LANES), lambda b, t: (b, 0, 0)),
                   pl.BlockSpec((1, 1, SSM_LANES), lambda b, t: (b, 0, 0))],
        out_shape=[jax.ShapeDtypeStruct((bsz, tp, w), F32), state, state],
        scratch_shapes=[pltpu.VMEM((tt, SSM_LANES), F32)] * 4
        + [pltpu.VMEM((8, SUBLANES, SSM_LANES), F32), pltpu.VMEM((2, SUBLANES, SSM_LANES), F32)],
        compiler_params=_params(("parallel", "arbitrary")),
        name="ssm_prompt",
    )(u, par, *mats)


def _ssm_sample_kernel(u_ref, h0r_ref, h0i_ref, par_ref, bre_ref, bim_ref, cre_ref, cim_ref, d_ref, wglu_ref,
                       o_ref, hre_ref, him_ref, hr_ref, hi_ref, *, nb, steps):
    u = u_ref[...]
    x_re, x_im = _ssm_inputs(u, par_ref, bre_ref, bim_ref)
    ab_re, ab_im = par_ref[0:1, :], par_ref[1:2, :]
    hr, hi = h0r_ref[...], h0i_ref[...]
    for s in range(steps):
        dr, di = _cmul(ab_re, ab_im, hr, hi)
        hr = dr + x_re[s * nb:(s + 1) * nb]
        hi = di + x_im[s * nb:(s + 1) * nb]
        hr_ref[s * nb:(s + 1) * nb, :] = hr
        hi_ref[s * nb:(s + 1) * nb, :] = hi
    hre_ref[...] = hr
    him_ref[...] = hi
    o_ref[...] = _ssm_outputs(u, hr_ref[...], hi_ref[...], cre_ref, cim_ref, d_ref, wglu_ref)


def _ssm_sample(u_tm, h0r, h0i, par, mats, nb, steps):
    rows, w = u_tm.shape
    state = jax.ShapeDtypeStruct((nb, SSM_LANES), F32)
    return pl.pallas_call(
        functools.partial(_ssm_sample_kernel, nb=nb, steps=steps),
        out_shape=[jax.ShapeDtypeStruct((rows, w), F32), state, state],
        scratch_shapes=[pltpu.VMEM((rows, SSM_LANES), F32)] * 2,
        compiler_params=pltpu.CompilerParams(vmem_limit_bytes=VMEM_LIMIT),
        name="ssm_sample",
    )(u_tm, h0r, h0i, par, *mats)


def _sample_attn_kernel(pt_ref, qa_ref, qc_ref, nka_ref, nva_ref, nkc_ref, nvc_ref,
                        pka_ref, pva_ref, pkc_ref, pvc_ref, u2_ref, lqk_ref, subln_ref,
                        osb_ref, odf_ref,
                        acc_a, carry_a, m_c, l_c, acc_c, *, steps, lam_init):
    del pt_ref
    p = pl.program_id(1)
    rows_a = SB_HEADS * SUBLANES
    rows_c = 2 * DIFF_HEADS * SUBLANES
    rows_h = 2 * SUBLANES
    qa = qa_ref[0]
    qc = qc_ref[0]
    u2 = u2_ref[...]

    def sb_block(kt, vt, mask, first):
        z = _dot(qa, kt)
        log_keep = -(jnp.maximum(z, 0.0) + jnp.log(1.0 + jnp.exp(-jnp.abs(z))))
        if mask is not None:
            log_keep = jnp.where(mask, log_keep, 0.0)
        hi = log_keep.astype(BF16)
        lo = (log_keep - hi.astype(F32)).astype(BF16)
        sums = _dot(jnp.concatenate([hi, lo], axis=1), u2)
        carry = jnp.zeros((rows_a, TB), F32) if first else carry_a[...]
        w = jnp.exp(z + log_keep + sums[:, :TB] + carry)
        if mask is not None:
            w = jnp.where(mask, w, 0.0)
        out = _dot_nt(w.astype(BF16), vt)
        acc_a[...] = out if first else acc_a[...] + out
        carry_a[...] = carry + sums[:, TB:]

    def diff_block(kt, v_of_head, mask, first):
        s = _dot(qc, kt)
        if mask is not None:
            s = jnp.where(mask, s, NEG_INF)
        m_prev = jnp.full((rows_c, 1), NEG_INF, F32) if first else m_c[...]
        m_new = jnp.maximum(m_prev, jnp.max(s, axis=1, keepdims=True))
        alpha = jnp.exp(m_prev - m_new)
        pr = jnp.exp(s - m_new)
        psum = jnp.sum(pr, axis=1, keepdims=True)
        m_c[...] = m_new
        l_c[...] = psum if first else alpha * l_c[...] + psum
        pr = pr.astype(BF16)
        for h in range(DIFF_HEADS):
            rows = slice(h * rows_h, (h + 1) * rows_h)
            out = _dot(pr[rows], v_of_head(h))
            acc_c[h] = out if first else alpha[rows] * acc_c[h] + out

    def new_rows(ref):
        x = ref[0]
        return jnp.concatenate([x, jnp.zeros((TB - SUBLANES, x.shape[1]), F32)], axis=0)

    @pl.when(p == 0)
    def _new_tokens():
        tok_a = lax.broadcasted_iota(jnp.int32, (rows_a, TB), 0) % SUBLANES
        key_a = lax.broadcasted_iota(jnp.int32, (rows_a, TB), 1)
        sb_block(new_rows(nka_ref).T.astype(BF16), new_rows(nva_ref).T.astype(BF16),
                 (key_a < tok_a) & (key_a < steps), True)
        tok_c = lax.broadcasted_iota(jnp.int32, (rows_c, TB), 0) % SUBLANES
        key_c = lax.broadcasted_iota(jnp.int32, (rows_c, TB), 1)
        nvc = new_rows(nvc_ref).astype(BF16)
        diff_block(new_rows(nkc_ref).T.astype(BF16),
                   lambda h: nvc[:, h * DIFF_V_DIM:(h + 1) * DIFF_V_DIM],
                   (key_c <= tok_c) & (key_c < steps), True)

    sb_block(pka_ref[0, 0].astype(BF16), pva_ref[0, 0].astype(BF16), None, False)
    diff_block(pkc_ref[0, 0].astype(BF16),
               lambda h: pvc_ref[0, 0, pl.ds(h, TB, stride=DIFF_HEADS), :].astype(BF16), None, False)

    @pl.when(p == pl.num_programs(1) - 1)
    def _finish():
        acc = acc_a[...]
        col = lax.broadcasted_iota(jnp.int32, (SUBLANES, SB_WIDTH), 1)
        out = jnp.zeros((SUBLANES, SB_WIDTH), F32)
        for h in range(SB_HEADS):
            own = (col >= h * SB_HEAD_DIM) & (col < (h + 1) * SB_HEAD_DIM)
            out = jnp.where(own, acc[h * SUBLANES:(h + 1) * SUBLANES], out)
        osb_ref[0] = out
        lam = _diff_lambda(lqk_ref, lam_init)
        l = l_c[...]
        for h in range(DIFF_HEADS):
            acc = acc_c[h]
            r0 = h * rows_h
            odf_ref[0, :, h * DIFF_V_DIM:(h + 1) * DIFF_V_DIM] = _diff_combine(
                acc[:SUBLANES], l[r0:r0 + SUBLANES], acc[SUBLANES:], l[r0 + SUBLANES:r0 + rows_h],
                lam, subln_ref[...], lam_init)


def _sample_attn(layer, page_table, qa, qc, nka, nva, nkc, nvc, cka, cva, ckc, cvc, u2, lqk, subln,
                 steps, lam_init):
    nb, n_pages = page_table.shape
    rows_a, rows_c = qa.shape[1], qc.shape[1]

    def per_batch(a):
        return pl.BlockSpec((1,) + a.shape[1:], lambda b, p, pt: (b, 0, 0))

    def paged(a):
        return pl.BlockSpec((1, 1) + a.shape[2:], lambda b, p, pt: (layer, pt[b, n_pages - 1 - p], 0, 0))

    def const(a):
        return pl.BlockSpec(a.shape, lambda b, p, pt: (0,) * a.ndim)

    grid_spec = pltpu.PrefetchScalarGridSpec(
        num_scalar_prefetch=1,
        grid=(nb, n_pages),
        in_specs=[per_batch(qa), per_batch(qc), per_batch(nka), per_batch(nva), per_batch(nkc), per_batch(nvc),
                  paged(cka), paged(cva), paged(ckc), paged(cvc), const(u2), const(lqk), const(subln)],
        out_specs=[pl.BlockSpec((1, SUBLANES, SB_WIDTH), lambda b, p, pt: (b, 0, 0)),
                   pl.BlockSpec((1, SUBLANES, DIFF_WIDTH), lambda b, p, pt: (b, 0, 0))],
        scratch_shapes=[pltpu.VMEM((rows_a, SB_WIDTH), F32),
                        pltpu.VMEM((rows_a, TB), F32),
                        pltpu.VMEM((rows_c, 1), F32),
                        pltpu.VMEM((rows_c, 1), F32),
                        pltpu.VMEM((DIFF_HEADS, 2 * SUBLANES, DIFF_V_DIM), F32)])
    return pl.pallas_call(
        functools.partial(_sample_attn_kernel, steps=steps, lam_init=lam_init),
        grid_spec=grid_spec,
        out_shape=[jax.ShapeDtypeStruct((nb, SUBLANES, SB_WIDTH), F32),
                   jax.ShapeDtypeStruct((nb, SUBLANES, DIFF_WIDTH), F32)],
        compiler_params=_params(("parallel", "arbitrary")),
        name="sample_attn",
    )(page_table, qa, qc, nka, nva, nkc, nvc, cka, cva, ckc, cvc, u2, lqk, subln)


def _rope_tables(pos):
    half = DIFF_QK_DIM // 2
    inv = ROPE_THETA ** (-2.0 * jnp.arange(half, dtype=F32) / DIFF_QK_DIM)
    ang = pos.astype(F32)[:, None] * inv[None, :]
    cos, sin = jnp.cos(ang), jnp.sin(ang)
    reps = LANES // DIFF_QK_DIM
    return (jnp.tile(jnp.concatenate([cos, cos], axis=1), (1, reps)),
            jnp.tile(jnp.concatenate([-sin, sin], axis=1), (1, reps)))


def _suffix_sum_matrix():
    r = lax.broadcasted_iota(jnp.int32, (2 * TB, 2 * TB), 0) % TB
    c = lax.broadcasted_iota(jnp.int32, (2 * TB, 2 * TB), 1)
    return jnp.where((c >= TB) | (r > c), 1.0, 0.0).astype(BF16)


def _block_diag(w, transpose):
    if transpose:
        w = jnp.swapaxes(w, 1, 2)
    g, a, b = w.shape
    eye = jnp.eye(g, dtype=w.dtype)
    return (w[:, :, None, :] * eye[:, None, :, None]).reshape(g * a, g * b)


def _per_batch_heads(x, nb, steps, groups, width):
    x = x.reshape(nb, steps, groups * width)
    x = jnp.pad(x, ((0, 0), (0, SUBLANES - steps), (0, 0)))
    col_group = jnp.arange(groups * width) // width
    own = col_group[None, :] == jnp.arange(groups)[:, None]
    out = jnp.where(own[None, :, None, :], x[:, None, :, :], jnp.zeros((), x.dtype))
    return out.reshape(nb, groups * SUBLANES, groups * width)


def _new_rows(x, nb, steps):
    x = x.reshape(nb, steps, x.shape[-1])
    return jnp.pad(x, ((0, 0), (0, SUBLANES - steps), (0, 0)))


def kernel(x_prompt, x_sample, cache_sb_k, cache_sb_v, cache_diff_k, cache_diff_v, state_ssm_re, state_ssm_im,
           page_table, meta_tokens, ln_in_g, ln_in_b, w_in, w_out, ln_g, ln_b, ssm_a_re, ssm_a_im, ssm_log_dt,
           ssm_b_re, ssm_b_im, ssm_c_re, ssm_c_im, ssm_d, ssm_w_glu, diff_lq1, diff_lk1, diff_lq2, diff_lk2,
           diff_subln):
    depth = w_in.shape[0]
    bsz, seq, d = x_prompt.shape
    nb, steps, _ = x_sample.shape
    n_pool, page = cache_sb_k.shape[1], cache_sb_k.shape[2]
    past_len = page_table.shape[1] * page
    assert page == TB and steps <= SUBLANES
    alpha = (2 * depth) ** 0.25
    t_real = seq + N_META
    pad = (-t_real) % TB
    tp = t_real + pad

    meta = jnp.broadcast_to(meta_tokens.astype(F32)[None], (bsz, N_META, d))
    xp = jnp.concatenate([jnp.zeros((bsz, pad, d), F32), meta, x_prompt], axis=1).reshape(bsz * tp, d)
    hp = _ln_rows(xp, ln_in_g, ln_in_b)
    hs = _ln_rows(x_sample.reshape(nb * steps, d), ln_in_g, ln_in_b)

    pos_p = jnp.maximum(jnp.arange(tp, dtype=jnp.int32) - pad, 0)
    cos_p, sin_p = (jnp.tile(a, (bsz, 1)) for a in _rope_tables(pos_p))
    cos_s, sin_s = (jnp.tile(a, (nb, 1)) for a in _rope_tables(past_len + jnp.arange(steps, dtype=jnp.int32)))
    u2 = _suffix_sum_matrix()

    def feature_major(c):
        return jnp.transpose(c, (0, 1, 3, 4, 2)).reshape(depth, n_pool, -1, page)

    cka, cva, ckc = feature_major(cache_sb_k), feature_major(cache_sb_v), feature_major(cache_diff_k)
    cvc = cache_diff_v.reshape(depth, n_pool, page * DIFF_HEADS, DIFF_V_DIM)

    rows_p = [[] for _ in range(6)]
    rows_s = [[] for _ in range(6)]
    for l in range(depth):
        lam_init = 0.8 - 0.6 * math.exp(-0.3 * l)
        w_in_bf = w_in[l].astype(BF16)
        w_out_bf = w_out[l].astype(BF16)
        par = _ssm_params(ssm_a_re[l], ssm_a_im[l], ssm_log_dt[l])
        mats = (_block_diag(ssm_b_re[l], True).astype(BF16), _block_diag(ssm_b_im[l], True).astype(BF16),
                _block_diag(ssm_c_re[l], True).astype(BF16), _block_diag(ssm_c_im[l], True).astype(BF16),
                ssm_d[l].reshape(1, SSM_WIDTH), ssm_w_glu[l].astype(BF16))
        lqk = jnp.stack([diff_lq1[l], diff_lk1[l], diff_lq2[l], diff_lk2[l]])
        subln = diff_subln[l].reshape(1, DIFF_V_DIM)

        qa, ka, va, kab, vab, sg, ub, qc, kc, vc, kcb, vcb = _proj(hp, w_in_bf, cos_p, sin_p)
        sb = _sb_prompt(qa.reshape(bsz, tp, -1), kab.reshape(bsz, tp, -1), vab.reshape(bsz, tp, -1), u2, pad)
        df = _diff_prompt(qc.reshape(bsz, tp, -1), kcb.reshape(bsz, tp, -1), vcb.reshape(bsz, tp, -1),
                          lqk, subln, pad, lam_init)
        ssm, hre, him = _ssm_prompt(ub.reshape(bsz, tp, -1), par, mats, pad)
        hp = _out_ln(sb.reshape(bsz * tp, -1), ssm.reshape(bsz * tp, -1), df.reshape(bsz * tp, -1), sg, hp,
                     w_out_bf, ln_g[l], ln_b[l], alpha)
        for lst, r in zip(rows_p, (ka, va, kc, vc)):
            lst.append(r.reshape(bsz, tp, -1)[:, pad:])
        rows_p[4].append(hre.reshape(bsz, SSM_GROUPS, SSM_STATE))
        rows_p[5].append(him.reshape(bsz, SSM_GROUPS, SSM_STATE))

        qa, ka, va, _, _, sg, ub, qc, kc, vc, _, _ = _proj(hs, w_in_bf, cos_s, sin_s)
        sb, df = _sample_attn(
            l, page_table,
            _per_batch_heads(qa, nb, steps, SB_HEADS, SB_HEAD_DIM),
            _per_batch_heads(qc, nb, steps, 2 * DIFF_HEADS, DIFF_QK_DIM),
            _new_rows(ka, nb, steps), _new_rows(va, nb, steps), _new_rows(kc, nb, steps), _new_rows(vc, nb, steps),
            cka, cva, ckc, cvc, u2, lqk, subln, steps, lam_init)
        sb = sb[:, :steps].reshape(nb * steps, -1)
        df = df[:, :steps].reshape(nb * steps, -1)
        u_tm = ub.reshape(nb, steps, -1).swapaxes(0, 1).reshape(steps * nb, -1)
        ssm_tm, hre, him = _ssm_sample(u_tm, state_ssm_re[l].reshape(nb, SSM_LANES),
                                       state_ssm_im[l].reshape(nb, SSM_LANES), par, mats, nb, steps)
        ssm = ssm_tm.reshape(steps, nb, -1).swapaxes(0, 1).reshape(nb * steps, -1)
        hs = _out_ln(sb, ssm, df, sg, hs, w_out_bf, ln_g[l], ln_b[l], alpha)
        for lst, r in zip(rows_s, (ka, va, kc, vc)):
            lst.append(r.reshape(nb, steps, -1))
        rows_s[4].append(hre.reshape(nb, SSM_GROUPS, SSM_STATE))
        rows_s[5].append(him.reshape(nb, SSM_GROUPS, SSM_STATE))

    def stack(lst, heads):
        a = jnp.stack(lst)
        return a.reshape(a.shape[:3] + (heads, a.shape[3] // heads))

    y_prompt = hp.reshape(bsz, tp, d)[:, pad + N_META:]
    y_sample = hs.reshape(nb, steps, d)
    outs = []
    for rows in (rows_p, rows_s):
        outs += [stack(rows[0], SB_HEADS), stack(rows[1], SB_HEADS), stack(rows[2], 2 * DIFF_HEADS),
                 stack(rows[3], DIFF_HEADS), jnp.stack(rows[4]), jnp.stack(rows[5])]
    return (y_prompt, y_sample, *outs)
```

```python
import functools
import math

import jax
import jax.numpy as jnp
from jax import lax
from jax.experimental import pallas as pl
from jax.experimental.pallas import tpu as pltpu

F32 = jnp.float32
BF16 = jnp.bfloat16

N_META = 16
LN_EPS = 1e-5
ROPE_THETA = 10000.0
NEG_INF = -1e30

SB_HEADS = 4
SB_HEAD_DIM = 64
SB_WIDTH = SB_HEADS * SB_HEAD_DIM
SSM_GROUPS = 16
SSM_GROUP = 16
SSM_STATE = 64
SSM_WIDTH = SSM_GROUPS * SSM_GROUP
SSM_LANES = SSM_GROUPS * SSM_STATE
DIFF_HEADS = 4
DIFF_QK_DIM = 64
DIFF_V_DIM = 128
DIFF_WIDTH = DIFF_HEADS * DIFF_V_DIM
QK_SCALE = 0.125 * math.log2(math.e)

LANES = 128
SUBLANES = 8
TB = 128
TQ = 2 * TB
PAGES_PER_STEP = 8
ROW_TILE = 512
VMEM_LIMIT = 56 * 1024 * 1024

_NT = (((1,), (1,)), ((), ()))


def _params(sem):
    return pltpu.CompilerParams(dimension_semantics=sem, vmem_limit_bytes=VMEM_LIMIT)


def _dot(a, b):
    return jnp.dot(a, b, preferred_element_type=F32)


def _dot_nt(a, b):
    return lax.dot_general(a, b, _NT, preferred_element_type=F32)


def _sigmoid(x):
    return 1.0 / (1.0 + jnp.exp(-x))


def _layer_norm(x, g, b):
    xc = x - jnp.mean(x, axis=-1, keepdims=True)
    var = jnp.mean(xc * xc, axis=-1, keepdims=True)
    return xc * lax.rsqrt(var + LN_EPS) * g + b


def _row_tile(rows):
    for t in (ROW_TILE, 384, 256, LANES):
        if rows % t == 0:
            return t
    raise ValueError(f"row count {rows} is not a multiple of {LANES}")


def _ln_kernel(x_ref, g_ref, b_ref, o_ref):
    o_ref[...] = _layer_norm(x_ref[...], g_ref[...], b_ref[...])


def _ln_rows(x, g, b):
    rows, d = x.shape
    tm = _row_tile(rows)
    return pl.pallas_call(
        _ln_kernel,
        grid=(rows // tm,),
        in_specs=[pl.BlockSpec((tm, d), lambda i: (i, 0)),
                  pl.BlockSpec((1, d), lambda i: (0, 0)),
                  pl.BlockSpec((1, d), lambda i: (0, 0))],
        out_specs=pl.BlockSpec((tm, d), lambda i: (i, 0)),
        out_shape=jax.ShapeDtypeStruct((rows, d), F32),
        compiler_params=_params(("parallel",)),
        name="ln_in",
    )(x, g.reshape(1, d), b.reshape(1, d))


def _rope(x, cos, sin_signed):
    lane = lax.broadcasted_iota(jnp.int32, (x.shape[0], LANES), 1)
    first_half = (lane % DIFF_QK_DIM) < (DIFF_QK_DIM // 2)
    outs = []
    for c in range(x.shape[1] // LANES):
        xc = x[:, c * LANES:(c + 1) * LANES]
        partner = jnp.where(first_half,
                            pltpu.roll(xc, LANES - DIFF_QK_DIM // 2, axis=1),
                            pltpu.roll(xc, DIFF_QK_DIM // 2, axis=1))
        outs.append(xc * cos + partner * sin_signed)
    return jnp.concatenate(outs, axis=1)


_C_QA, _C_KA, _C_VA, _C_GA = 0, 256, 512, 768
_C_UB, _C_GB = 1024, 1280
_C_QC, _C_KC, _C_VC, _C_GC, _C_END = 1536, 2048, 2560, 3072, 3584


def _proj_kernel(h_ref, w_ref, cos_ref, sin_ref,
                 qa_ref, ka_ref, va_ref, kab_ref, vab_ref, sg_ref, ub_ref,
                 qc_ref, kc_ref, vc_ref, kcb_ref, vcb_ref):
    hb = h_ref[...].astype(BF16)

    def mm(c0, c1):
        return _dot(hb, w_ref[:, c0:c1])

    def silu(g):
        return g * _sigmoid(g)

    cos = cos_ref[...]
    sin = sin_ref[...]
    qa_ref[...] = (mm(_C_QA, _C_KA) * QK_SCALE).astype(BF16)
    ka = mm(_C_KA, _C_VA)
    ka_ref[...] = ka
    kab_ref[...] = ka.astype(BF16)
    va = mm(_C_VA, _C_GA)
    va_ref[...] = va
    vab_ref[...] = va.astype(BF16)
    sg_ref[:, 0:SB_WIDTH] = silu(mm(_C_GA, _C_UB))
    ub_ref[...] = mm(_C_UB, _C_GB)
    sg_ref[:, SB_WIDTH:SB_WIDTH + SSM_WIDTH] = silu(mm(_C_GB, _C_QC))
    qc_ref[...] = (_rope(mm(_C_QC, _C_KC), cos, sin) * QK_SCALE).astype(BF16)
    kc = _rope(mm(_C_KC, _C_VC), cos, sin)
    kc_ref[...] = kc
    kcb_ref[...] = kc.astype(BF16)
    vc = mm(_C_VC, _C_GC)
    vc_ref[...] = vc
    vcb_ref[...] = vc.astype(BF16)
    sg_ref[:, SB_WIDTH + SSM_WIDTH:] = silu(mm(_C_GC, _C_END))


def _proj(h, w_bf, cos, sin):
    rows, d = h.shape
    tm = _row_tile(rows)
    widths = [(SB_WIDTH, BF16), (SB_WIDTH, F32), (SB_WIDTH, F32), (SB_WIDTH, BF16), (SB_WIDTH, BF16),
              (d, F32), (SSM_WIDTH, F32),
              (DIFF_WIDTH, BF16), (DIFF_WIDTH, F32), (DIFF_WIDTH, F32), (DIFF_WIDTH, BF16), (DIFF_WIDTH, BF16)]
    return pl.pallas_call(
        _proj_kernel,
        grid=(rows // tm,),
        in_specs=[pl.BlockSpec((tm, d), lambda i: (i, 0)),
                  pl.BlockSpec(w_bf.shape, lambda i: (0, 0)),
                  pl.BlockSpec((tm, LANES), lambda i: (i, 0)),
                  pl.BlockSpec((tm, LANES), lambda i: (i, 0))],
        out_specs=[pl.BlockSpec((tm, w), lambda i: (i, 0)) for w, _ in widths],
        out_shape=[jax.ShapeDtypeStruct((rows, w), dt) for w, dt in widths],
        compiler_params=_params(("parallel",)),
        name="proj",
    )(h, w_bf, cos, sin)


def _out_kernel(sb_ref, ssm_ref, df_ref, sg_ref, h_ref, w_ref, g_ref, b_ref, o_ref, *, alpha):
    mixed = jnp.concatenate([sb_ref[...], ssm_ref[...], df_ref[...]], axis=1) * sg_ref[...]
    out = _dot(mixed.astype(BF16), w_ref[...])
    o_ref[...] = _layer_norm(alpha * h_ref[...] + out, g_ref[...], b_ref[...])


def _out_ln(sb, ssm, df, sg, h, w_bf, g, b, alpha):
    rows, d = h.shape
    tm = _row_tile(rows)

    def rowspec(w):
        return pl.BlockSpec((tm, w), lambda i: (i, 0))

    return pl.pallas_call(
        functools.partial(_out_kernel, alpha=alpha),
        grid=(rows // tm,),
        in_specs=[rowspec(SB_WIDTH), rowspec(SSM_WIDTH), rowspec(DIFF_WIDTH), rowspec(d), rowspec(d),
                  pl.BlockSpec(w_bf.shape, lambda i: (0, 0)),
                  pl.BlockSpec((1, d), lambda i: (0, 0)),
                  pl.BlockSpec((1, d), lambda i: (0, 0))],
        out_specs=rowspec(d),
        out_shape=jax.ShapeDtypeStruct((rows, d), F32),
        compiler_params=_params(("parallel",)),
        name="out_ln",
    )(sb, ssm, df, sg, h, w_bf, g.reshape(1, d), b.reshape(1, d))


def _sb_weights(z, u2, carry, mask):
    log_keep = -(jnp.maximum(z, 0.0) + jnp.log2(1.0 + jnp.exp2(-jnp.abs(z))))
    if mask is not None:
        log_keep = jnp.where(mask, log_keep, 0.0)
    hi = log_keep.astype(BF16)
    lo = (log_keep - hi.astype(F32)).astype(BF16)
    rows, nblk = z.shape[0], z.shape[1] // TB
    blocks = [slice(n * TB, (n + 1) * TB) for n in range(nblk)]
    sums = _dot(jnp.concatenate([jnp.concatenate([hi[:, b], lo[:, b]], axis=1) for b in blocks], axis=0), u2)
    args = [None] * nblk
    for n in reversed(range(nblk)):
        blk = sums[n * rows:(n + 1) * rows]
        args[n] = z[:, blocks[n]] + blk[:, :TB] + carry
        carry = carry + blk[:, TB:]
    w = jnp.exp2(jnp.concatenate(args, axis=1))
    if mask is not None:
        w = jnp.where(mask, w, 0.0)
    return w, carry


def _head_masked(x, heads, width):
    col = lax.broadcasted_iota(jnp.int32, x.shape, 1)
    return [jnp.where((col >= h * width) & (col < (h + 1) * width), x, 0.0).astype(BF16)
            for h in range(heads)]


def _causal_sweep(i, step, diag_mask, pad_mask):
    step(i, diag_mask, True)

    def body(t, c):
        step(i - 1 - t, None, False)
        return c
    lax.fori_loop(0, jnp.maximum(i - 1, 0), body, 0)

    @pl.when(i > 0)
    def _first_chunk():
        step(0, pad_mask, False)


def _sb_prompt_kernel(q_ref, k_ref, v_ref, u2_ref, o_ref, vbd_ref, acc_ref, carry_ref, *, pad, nchunk):
    i = pl.program_id(1)

    @pl.when(i == 0)
    def _build_values():
        def build(c, carry):
            vb = v_ref[0, pl.ds(pl.multiple_of(c * TQ, TQ), TQ), :].astype(F32)
            for h, vm in enumerate(_head_masked(vb, SB_HEADS, SB_HEAD_DIM)):
                vbd_ref[c, h * TQ:(h + 1) * TQ, :] = vm
            return carry
        lax.fori_loop(0, nchunk, build, 0)

    qm = _head_masked(q_ref[0].astype(F32), SB_HEADS, SB_HEAD_DIM)
    row = lax.broadcasted_iota(jnp.int32, (TQ, TQ), 0)
    col = lax.broadcasted_iota(jnp.int32, (TQ, TQ), 1)
    u2 = u2_ref[...]

    def step(c, mask, first):
        kb = k_ref[0, pl.ds(pl.multiple_of(c * TQ, TQ), TQ), :]
        ws = []
        for h in range(SB_HEADS):
            carry = jnp.zeros((TQ, TB), F32) if first else carry_ref[h]
            w, carry_ref[h] = _sb_weights(_dot_nt(qm[h], kb), u2, carry, mask)
            ws.append(w.astype(BF16))
        out = _dot(jnp.concatenate(ws, axis=1), vbd_ref[c])
        acc_ref[...] = out if first else acc_ref[...] + out

    _causal_sweep(i, step, (col < row) & (i * TQ + col >= pad), col >= pad)
    o_ref[0] = acc_ref[...]


def _sb_prompt(q, k, v, u2, pad):
    bsz, tp, w = q.shape
    nchunk = tp // TQ
    return pl.pallas_call(
        functools.partial(_sb_prompt_kernel, pad=pad, nchunk=nchunk),
        grid=(bsz, nchunk),
        in_specs=[pl.BlockSpec((1, TQ, w), lambda b, i: (b, i, 0)),
                  pl.BlockSpec((1, tp, w), lambda b, i: (b, 0, 0)),
                  pl.BlockSpec((1, tp, w), lambda b, i: (b, 0, 0)),
                  pl.BlockSpec(u2.shape, lambda b, i: (0, 0))],
        out_specs=pl.BlockSpec((1, TQ, w), lambda b, i: (b, i, 0)),
        out_shape=jax.ShapeDtypeStruct((bsz, tp, w), F32),
        scratch_shapes=[pltpu.VMEM((nchunk, SB_HEADS * TQ, w), BF16),
                        pltpu.VMEM((TQ, w), F32),
                        pltpu.VMEM((SB_HEADS, TQ, TB), F32)],
        compiler_params=_params(("parallel", "arbitrary")),
        name="sb_prompt",
    )(q, k, v, u2)


def _diff_lambda(lqk_ref, lam_init):
    lqk = lqk_ref[...]
    s1 = jnp.sum(lqk[0:1] * lqk[1:2], axis=1, keepdims=True)
    s2 = jnp.sum(lqk[2:3] * lqk[3:4], axis=1, keepdims=True)
    return jnp.exp(s1) - jnp.exp(s2) + lam_init


def _diff_combine(acc0, l0, acc1, l1, lam, subln, lam_init):
    d = acc0 / l0 - lam * (acc1 / l1)
    return d * lax.rsqrt(jnp.mean(d * d, axis=1, keepdims=True) + LN_EPS) * subln * (1.0 - lam_init)


def _diff_prompt_kernel(q_ref, k_ref, v_ref, lqk_ref, subln_ref, o_ref,
                        qs_ref, mx_ref, l_ref, acc_ref, *, pad, lam_init):
    i = pl.program_id(1)
    half = 2 * DIFF_QK_DIM * 2
    qf = q_ref[0].astype(F32)
    for hh in range(DIFF_HEADS // 2):
        maps = _head_masked(qf[:, hh * half:(hh + 1) * half], 4, DIFF_QK_DIM)
        for hl in range(2):
            h = hh * 2 + hl
            qs_ref[h, 0:TQ, :] = maps[hl * 2]
            qs_ref[h, TQ:2 * TQ, :] = maps[hl * 2 + 1]
    row = lax.broadcasted_iota(jnp.int32, (2 * TQ, TQ), 0) % TQ
    col = lax.broadcasted_iota(jnp.int32, (2 * TQ, TQ), 1)
    diag_mask = (col <= row) & (i * TQ + col >= pad)
    pad_mask = col >= pad

    def scores(c, h, mask):
        kb = k_ref[0, pl.ds(pl.multiple_of(c * TQ, TQ), TQ), (h // 2) * half:(h // 2 + 1) * half]
        s = _dot_nt(qs_ref[h], kb)
        return s if mask is None else jnp.where(mask, s, NEG_INF)

    def max_step(c, mask, first):
        for h in range(DIFF_HEADS):
            s = scores(c, h, mask)
            m = jnp.maximum(s[:, :LANES], s[:, LANES:])
            mx_ref[h] = m if first else jnp.maximum(mx_ref[h], m)

    _causal_sweep(i, max_step, diag_mask, pad_mask)
    for h in range(DIFF_HEADS):
        mx_ref[h] = jnp.broadcast_to(jnp.max(mx_ref[h], axis=1, keepdims=True), (2 * TQ, LANES))

    def sum_step(c, mask, first):
        for h in range(DIFF_HEADS):
            m = mx_ref[h]
            s = scores(c, h, mask)
            p0 = jnp.exp2(s[:, :LANES] - m)
            p1 = jnp.exp2(s[:, LANES:] - m)
            vb = v_ref[0, pl.ds(pl.multiple_of(c * TQ, TQ), TQ), h * DIFF_V_DIM:(h + 1) * DIFF_V_DIM]
            pv = _dot(jnp.concatenate([p0, p1], axis=1).astype(BF16), vb)
            l_ref[h] = p0 + p1 if first else l_ref[h] + (p0 + p1)
            acc_ref[h] = pv if first else acc_ref[h] + pv

    _causal_sweep(i, sum_step, diag_mask, pad_mask)

    lam = _diff_lambda(lqk_ref, lam_init)
    for h in range(DIFF_HEADS):
        acc = acc_ref[h]
        l = jnp.sum(l_ref[h], axis=1, keepdims=True)
        o_ref[0, :, h * DIFF_V_DIM:(h + 1) * DIFF_V_DIM] = _diff_combine(
            acc[:TQ], l[:TQ], acc[TQ:], l[TQ:], lam, subln_ref[...], lam_init)


def _diff_prompt(q, k, v, lqk, subln, pad, lam_init):
    bsz, tp, w = q.shape
    stacked = jax.ShapeDtypeStruct((DIFF_HEADS, 2 * TQ, LANES), F32)
    return pl.pallas_call(
        functools.partial(_diff_prompt_kernel, pad=pad, lam_init=lam_init),
        grid=(bsz, tp // TQ),
        in_specs=[pl.BlockSpec((1, TQ, w), lambda b, i: (b, i, 0)),
                  pl.BlockSpec((1, tp, w), lambda b, i: (b, 0, 0)),
                  pl.BlockSpec((1, tp, w), lambda b, i: (b, 0, 0)),
                  pl.BlockSpec(lqk.shape, lambda b, i: (0, 0)),
                  pl.BlockSpec(subln.shape, lambda b, i: (0, 0))],
        out_specs=pl.BlockSpec((1, TQ, w), lambda b, i: (b, i, 0)),
        out_shape=jax.ShapeDtypeStruct((bsz, tp, w), F32),
        scratch_shapes=[pltpu.VMEM((DIFF_HEADS, 2 * TQ, 4 * DIFF_QK_DIM), BF16),
                        pltpu.VMEM(stacked.shape, F32),
                        pltpu.VMEM(stacked.shape, F32),
                        pltpu.VMEM(stacked.shape, F32)],
        compiler_params=_params(("parallel", "arbitrary")),
        name="diff_prompt",
    )(q, k, v, lqk, subln)


def _ssm_param_kernel(are_ref, aim_ref, ldt_ref, o_ref):
    lam_re, lam_im = are_ref[...], aim_ref[...]
    dt = jnp.exp(ldt_ref[...])
    mag = jnp.exp(lam_re * dt)
    ab_re = mag * jnp.cos(lam_im * dt)
    ab_im = mag * jnp.sin(lam_im * dt)
    den = lam_re * lam_re + lam_im * lam_im
    nr = ab_re - 1.0
    o_ref[0] = ab_re
    o_ref[1] = ab_im
    o_ref[2] = (nr * lam_re + ab_im * lam_im) / den
    o_ref[3] = (ab_im * lam_re - nr * lam_im) / den


def _ssm_params(a_re, a_im, log_dt):
    out = pl.pallas_call(
        _ssm_param_kernel,
        out_shape=jax.ShapeDtypeStruct((4, SSM_GROUPS, SSM_STATE), F32),
        name="ssm_params",
    )(a_re, a_im, log_dt.reshape(SSM_GROUPS, 1))
    return out.reshape(4, SSM_LANES)


def _cmul(ar, ai, br, bi):
    return ar * br - ai * bi, ar * bi + ai * br


def _ssm_inputs(u, par_ref, bre_ref, bim_ref):
    ub = u.astype(BF16)
    return _cmul(par_ref[2:3, :], par_ref[3:4, :], _dot(ub, bre_ref[...]), _dot(ub, bim_ref[...]))


def _ssm_outputs(u, h_re, h_im, cre_ref, cim_ref, d_ref, wglu_ref):
    y = _dot(h_re.astype(BF16), cre_ref[...]) - _dot(h_im.astype(BF16), cim_ref[...]) + d_ref[...] * u
    y = 0.5 * y * (1.0 + jnp.tanh(math.sqrt(2.0 / math.pi) * (y + 0.044715 * (y * y * y))))
    gl = _dot(y.astype(BF16), wglu_ref[...])
    return gl[:, :SSM_WIDTH] * _sigmoid(gl[:, SSM_WIDTH:])


def _ssm_prompt_kernel(u_ref, par_ref, bre_ref, bim_ref, cre_ref, cim_ref, d_ref, wglu_ref,
                       o_ref, hre_ref, him_ref,
                       xr_ref, xi_ref, hr_ref, hi_ref, cst_ref, st_ref, *, pad, tt):
    t = pl.program_id(1)
    sub = lax.broadcasted_iota(jnp.int32, (SUBLANES, SSM_LANES), 0)

    @pl.when(t == 0)
    def _init():
        st_ref[...] = jnp.zeros_like(st_ref)
        pr, pi = [par_ref[0:1, :]], [par_ref[1:2, :]]
        for _ in range(SUBLANES - 1):
            nr, ni = _cmul(pr[-1], pi[-1], pr[0], pi[0])
            pr.append(nr)
            pi.append(ni)
        for n, k in enumerate((1, 2, 4)):
            cst_ref[2 * n] = jnp.where(sub >= k, pr[k - 1], 0.0)
            cst_ref[2 * n + 1] = jnp.where(sub >= k, pi[k - 1], 0.0)
        car_r = jnp.zeros((SUBLANES, SSM_LANES), F32)
        car_i = jnp.zeros((SUBLANES, SSM_LANES), F32)
        for s in range(SUBLANES):
            car_r = jnp.where(sub == s, pr[s], car_r)
            car_i = jnp.where(sub == s, pi[s], car_i)
        cst_ref[6] = car_r
        cst_ref[7] = car_i

    u = u_ref[0]
    x_re, x_im = _ssm_inputs(u, par_ref, bre_ref, bim_ref)
    live = (t * tt + lax.broadcasted_iota(jnp.int32, (tt, 1), 0)) >= pad
    xr_ref[...] = jnp.where(live, x_re, 0.0)
    xi_ref[...] = jnp.where(live, x_im, 0.0)

    def tile(n, carry):
        cr, ci = carry
        r0 = pl.multiple_of(n * SUBLANES, SUBLANES)
        hr = xr_ref[pl.ds(r0, SUBLANES), :]
        hi = xi_ref[pl.ds(r0, SUBLANES), :]
        for m, k in enumerate((1, 2, 4)):
            dr, di = _cmul(cst_ref[2 * m], cst_ref[2 * m + 1],
                           pltpu.roll(hr, k, axis=0), pltpu.roll(hi, k, axis=0))
            hr, hi = hr + dr, hi + di
        dr, di = _cmul(cst_ref[6], cst_ref[7], cr, ci)
        hr, hi = hr + dr, hi + di
        hr_ref[pl.ds(r0, SUBLANES), :] = hr
        hi_ref[pl.ds(r0, SUBLANES), :] = hi
        last = SUBLANES - 1
        return (jnp.broadcast_to(hr[last:last + 1, :], hr.shape),
                jnp.broadcast_to(hi[last:last + 1, :], hi.shape))

    cr, ci = lax.fori_loop(0, tt // SUBLANES, tile, (st_ref[0], st_ref[1]))
    st_ref[0] = cr
    st_ref[1] = ci
    hre_ref[0] = cr[0:1, :]
    him_ref[0] = ci[0:1, :]
    o_ref[0] = _ssm_outputs(u, hr_ref[...], hi_ref[...], cre_ref, cim_ref, d_ref, wglu_ref)


def _ssm_prompt(u, par, mats, pad):
    bsz, tp, w = u.shape
    tt = TQ

    def const(a):
        return pl.BlockSpec(a.shape, lambda b, t: (0,) * a.ndim)

    state = jax.ShapeDtypeStruct((bsz, 1, SSM_LANES), F32)
    return pl.pallas_call(
        functools.partial(_ssm_prompt_kernel, pad=pad, tt=tt),
        grid=(bsz, tp // tt),
        in_specs=[pl.BlockSpec((1, tt, w), lambda b, t: (b, t, 0)), const(par)] + [const(m) for m in mats],
        out_specs=[pl.BlockSpec((1, tt, w), lambda b, t: (b, t, 0)),
                   pl.BlockSpec((1, 1, SSM_LANES), lambda b, t: (b, 0, 0)),
                   pl.BlockSpec((1, 1, SSM_LANES), lambda b, t: (b, 0, 0))],
        out_shape=[jax.ShapeDtypeStruct((bsz, tp, w), F32), state, state],
        scratch_shapes=[pltpu.VMEM((tt, SSM_LANES), F32)] * 4
        + [pltpu.VMEM((8, SUBLANES, SSM_LANES), F32), pltpu.VMEM((2, SUBLANES, SSM_LANES), F32)],
        compiler_params=_params(("parallel", "arbitrary")),
        name="ssm_prompt",
    )(u, par, *mats)


def _ssm_sample_kernel(u_ref, h0r_ref, h0i_ref, par_ref, bre_ref, bim_ref, cre_ref, cim_ref, d_ref, wglu_ref,
                       o_ref, hre_ref, him_ref, hr_ref, hi_ref, *, nb, steps):
    u = u_ref[...]
    x_re, x_im = _ssm_inputs(u, par_ref, bre_ref, bim_ref)
    ab_re, ab_im = par_ref[0:1, :], par_ref[1:2, :]
    hr, hi = h0r_ref[...], h0i_ref[...]
    for s in range(steps):
        dr, di = _cmul(ab_re, ab_im, hr, hi)
        hr = dr + x_re[s * nb:(s + 1) * nb]
        hi = di + x_im[s * nb:(s + 1) * nb]
        hr_ref[s * nb:(s + 1) * nb, :] = hr
        hi_ref[s * nb:(s + 1) * nb, :] = hi
    hre_ref[...] = hr
    him_ref[...] = hi
    o_ref[...] = _ssm_outputs(u, hr_ref[...], hi_ref[...], cre_ref, cim_ref, d_ref, wglu_ref)


def _ssm_sample(u_tm, h0r, h0i, par, mats, nb, steps):
    rows, w = u_tm.shape
    state = jax.ShapeDtypeStruct((nb, SSM_LANES), F32)
    return pl.pallas_call(
        functools.partial(_ssm_sample_kernel, nb=nb, steps=steps),
        out_shape=[jax.ShapeDtypeStruct((rows, w), F32), state, state],
        scratch_shapes=[pltpu.VMEM((rows, SSM_LANES), F32)] * 2,
        compiler_params=pltpu.CompilerParams(vmem_limit_bytes=VMEM_LIMIT),
        name="ssm_sample",
    )(u_tm, h0r, h0i, par, *mats)


def _sample_attn_kernel(pt_ref, qa_ref, qc_ref, nka_ref, nva_ref, nkc_ref, nvc_ref, *rest, steps, lam_init):
    del pt_ref
    g = PAGES_PER_STEP
    pka, pva, pkc, pvc = (rest[n * g:(n + 1) * g] for n in range(4))
    u2_ref, lqk_ref, subln_ref, osb_ref, odf_ref, acc_a, carry_a, m_c, l_c, acc_c = rest[4 * g:]
    p = pl.program_id(1)
    rows_a = SB_HEADS * SUBLANES
    rows_c = 2 * DIFF_HEADS * SUBLANES
    rows_h = 2 * SUBLANES
    qa = qa_ref[0]
    qc = qc_ref[0]
    u2 = u2_ref[...]

    def sb_block(kt, vt, mask, first):
        carry = jnp.zeros((rows_a, TB), F32) if first else carry_a[...]
        w, carry_a[...] = _sb_weights(_dot(qa, kt), u2, carry, mask)
        out = _dot_nt(w.astype(BF16), vt)
        acc_a[...] = out if first else acc_a[...] + out

    def diff_block(kt, v_of_head, mask, first):
        s = _dot(qc, kt)
        if mask is not None:
            s = jnp.where(mask, s, NEG_INF)
        m_prev = jnp.full((rows_c, 1), NEG_INF, F32) if first else m_c[...]
        m_new = jnp.maximum(m_prev, jnp.max(s, axis=1, keepdims=True))
        alpha = jnp.exp2(m_prev - m_new)
        pr = jnp.exp2(s - m_new)
        psum = jnp.sum(pr, axis=1, keepdims=True)
        m_c[...] = m_new
        l_c[...] = psum if first else alpha * l_c[...] + psum
        pr = pr.astype(BF16)
        for h in range(DIFF_HEADS):
            rows = slice(h * rows_h, (h + 1) * rows_h)
            out = _dot(pr[rows], v_of_head(h))
            acc_c[h] = out if first else alpha[rows] * acc_c[h] + out

    def new_rows(ref):
        x = ref[0]
        return jnp.concatenate([x, jnp.zeros((TB - SUBLANES, x.shape[1]), F32)], axis=0)

    @pl.when(p == 0)
    def _new_tokens():
        tok_a = lax.broadcasted_iota(jnp.int32, (rows_a, TB), 0) % SUBLANES
        key_a = lax.broadcasted_iota(jnp.int32, (rows_a, TB), 1)
        sb_block(new_rows(nka_ref).T.astype(BF16), new_rows(nva_ref).T.astype(BF16),
                 (key_a < tok_a) & (key_a < steps), True)
        tok_c = lax.broadcasted_iota(jnp.int32, (rows_c, TB), 0) % SUBLANES
        key_c = lax.broadcasted_iota(jnp.int32, (rows_c, TB), 1)
        nvc = new_rows(nvc_ref).astype(BF16)
        diff_block(new_rows(nkc_ref).T.astype(BF16),
                   lambda h: nvc[:, h * DIFF_V_DIM:(h + 1) * DIFF_V_DIM],
                   (key_c <= tok_c) & (key_c < steps), True)

    def lane_cat(refs):
        return jnp.concatenate([r[0, 0].astype(BF16) for r in refs], axis=1)

    sb_block(lane_cat(pka), lane_cat(pva), None, False)
    diff_block(lane_cat(pkc),
               lambda h: jnp.concatenate(
                   [r[0, 0, pl.ds(h, TB, stride=DIFF_HEADS), :].astype(BF16) for r in pvc], axis=0),
               None, False)

    @pl.when(p == pl.num_programs(1) - 1)
    def _finish():
        acc = acc_a[...]
        col = lax.broadcasted_iota(jnp.int32, (SUBLANES, SB_WIDTH), 1)
        out = jnp.zeros((SUBLANES, SB_WIDTH), F32)
        for h in range(SB_HEADS):
            own = (col >= h * SB_HEAD_DIM) & (col < (h + 1) * SB_HEAD_DIM)
            out = jnp.where(own, acc[h * SUBLANES:(h + 1) * SUBLANES], out)
        osb_ref[0] = out
        lam = _diff_lambda(lqk_ref, lam_init)
        l = l_c[...]
        for h in range(DIFF_HEADS):
            acc = acc_c[h]
            r0 = h * rows_h
            odf_ref[0, :, h * DIFF_V_DIM:(h + 1) * DIFF_V_DIM] = _diff_combine(
                acc[:SUBLANES], l[r0:r0 + SUBLANES], acc[SUBLANES:], l[r0 + SUBLANES:r0 + rows_h],
                lam, subln_ref[...], lam_init)


def _sample_attn(layer, page_table, qa, qc, nka, nva, nkc, nvc, cka, cva, ckc, cvc, u2, lqk, subln,
                 steps, lam_init):
    nb, n_pages = page_table.shape
    rows_a, rows_c = qa.shape[1], qc.shape[1]
    g = PAGES_PER_STEP
    assert n_pages % g == 0

    def per_batch(a):
        return pl.BlockSpec((1,) + a.shape[1:], lambda b, p, pt: (b, 0, 0))

    def paged(a):
        return [pl.BlockSpec((1, 1) + a.shape[2:],
                             lambda b, p, pt, n=n: (layer, pt[b, n_pages - (p + 1) * g + n], 0, 0))
                for n in range(g)]

    def const(a):
        return pl.BlockSpec(a.shape, lambda b, p, pt: (0,) * a.ndim)

    caches = [cka] * g + [cva] * g + [ckc] * g + [cvc] * g
    grid_spec = pltpu.PrefetchScalarGridSpec(
        num_scalar_prefetch=1,
        grid=(nb, n_pages // g),
        in_specs=[per_batch(qa), per_batch(qc), per_batch(nka), per_batch(nva), per_batch(nkc), per_batch(nvc)]
        + paged(cka) + paged(cva) + paged(ckc) + paged(cvc) + [const(u2), const(lqk), const(subln)],
        out_specs=[pl.BlockSpec((1, SUBLANES, SB_WIDTH), lambda b, p, pt: (b, 0, 0)),
                   pl.BlockSpec((1, SUBLANES, DIFF_WIDTH), lambda b, p, pt: (b, 0, 0))],
        scratch_shapes=[pltpu.VMEM((rows_a, SB_WIDTH), F32),
                        pltpu.VMEM((rows_a, TB), F32),
                        pltpu.VMEM((rows_c, 1), F32),
                        pltpu.VMEM((rows_c, 1), F32),
                        pltpu.VMEM((DIFF_HEADS, 2 * SUBLANES, DIFF_V_DIM), F32)])
    return pl.pallas_call(
        functools.partial(_sample_attn_kernel, steps=steps, lam_init=lam_init),
        grid_spec=grid_spec,
        out_shape=[jax.ShapeDtypeStruct((nb, SUBLANES, SB_WIDTH), F32),
                   jax.ShapeDtypeStruct((nb, SUBLANES, DIFF_WIDTH), F32)],
        compiler_params=_params(("parallel", "arbitrary")),
        name="sample_attn",
    )(page_table, qa, qc, nka, nva, nkc, nvc, *caches, u2, lqk, subln)


def _rope_tables(pos):
    half = DIFF_QK_DIM // 2
    inv = ROPE_THETA ** (-2.0 * jnp.arange(half, dtype=F32) / DIFF_QK_DIM)
    ang = pos.astype(F32)[:, None] * inv[None, :]
    cos, sin = jnp.cos(ang), jnp.sin(ang)
    reps = LANES // DIFF_QK_DIM
    return (jnp.tile(jnp.concatenate([cos, cos], axis=1), (1, reps)),
            jnp.tile(jnp.concatenate([-sin, sin], axis=1), (1, reps)))


def _suffix_sum_matrix():
    r = lax.broadcasted_iota(jnp.int32, (2 * TB, 2 * TB), 0) % TB
    c = lax.broadcasted_iota(jnp.int32, (2 * TB, 2 * TB), 1)
    return jnp.where((c >= TB) | (r >= c), 1.0, 0.0).astype(BF16)


def _block_diag(w, transpose):
    if transpose:
        w = jnp.swapaxes(w, 1, 2)
    g, a, b = w.shape
    eye = jnp.eye(g, dtype=w.dtype)
    return (w[:, :, None, :] * eye[:, None, :, None]).reshape(g * a, g * b)


def _per_batch_heads(x, nb, steps, groups, width):
    x = x.reshape(nb, steps, groups * width)
    x = jnp.pad(x, ((0, 0), (0, SUBLANES - steps), (0, 0)))
    col_group = jnp.arange(groups * width) // width
    own = col_group[None, :] == jnp.arange(groups)[:, None]
    out = jnp.where(own[None, :, None, :], x[:, None, :, :], jnp.zeros((), x.dtype))
    return out.reshape(nb, groups * SUBLANES, groups * width)


def _new_rows(x, nb, steps):
    x = x.reshape(nb, steps, x.shape[-1])
    return jnp.pad(x, ((0, 0), (0, SUBLANES - steps), (0, 0)))


def kernel(x_prompt, x_sample, cache_sb_k, cache_sb_v, cache_diff_k, cache_diff_v, state_ssm_re, state_ssm_im,
           page_table, meta_tokens, ln_in_g, ln_in_b, w_in, w_out, ln_g, ln_b, ssm_a_re, ssm_a_im, ssm_log_dt,
           ssm_b_re, ssm_b_im, ssm_c_re, ssm_c_im, ssm_d, ssm_w_glu, diff_lq1, diff_lk1, diff_lq2, diff_lk2,
           diff_subln):
    depth = w_in.shape[0]
    bsz, seq, d = x_prompt.shape
    nb, steps, _ = x_sample.shape
    n_pool, page = cache_sb_k.shape[1], cache_sb_k.shape[2]
    past_len = page_table.shape[1] * page
    assert page == TB and steps <= SUBLANES
    alpha = (2 * depth) ** 0.25
    t_real = seq + N_META
    pad = (-t_real) % TQ
    tp = t_real + pad

    meta = jnp.broadcast_to(meta_tokens.astype(F32)[None], (bsz, N_META, d))
    xp = jnp.concatenate([jnp.zeros((bsz, pad, d), F32), meta, x_prompt], axis=1).reshape(bsz * tp, d)
    hp = _ln_rows(xp, ln_in_g, ln_in_b)
    hs = _ln_rows(x_sample.reshape(nb * steps, d), ln_in_g, ln_in_b)

    pos_p = jnp.maximum(jnp.arange(tp, dtype=jnp.int32) - pad, 0)
    cos_p, sin_p = (jnp.tile(a, (bsz, 1)) for a in _rope_tables(pos_p))
    cos_s, sin_s = (jnp.tile(a, (nb, 1)) for a in _rope_tables(past_len + jnp.arange(steps, dtype=jnp.int32)))
    u2 = _suffix_sum_matrix()

    def feature_major(c):
        return jnp.transpose(c, (0, 1, 3, 4, 2)).reshape(depth, n_pool, -1, page)

    cka, cva, ckc = feature_major(cache_sb_k), feature_major(cache_sb_v), feature_major(cache_diff_k)
    cvc = cache_diff_v.reshape(depth, n_pool, page * DIFF_HEADS, DIFF_V_DIM)

    rows_p = [[] for _ in range(6)]
    rows_s = [[] for _ in range(6)]
    for l in range(depth):
        lam_init = 0.8 - 0.6 * math.exp(-0.3 * l)
        w_in_bf = w_in[l].astype(BF16)
        w_out_bf = w_out[l].astype(BF16)
        par = _ssm_params(ssm_a_re[l], ssm_a_im[l], ssm_log_dt[l])
        mats = (_block_diag(ssm_b_re[l], True).astype(BF16), _block_diag(ssm_b_im[l], True).astype(BF16),
                _block_diag(ssm_c_re[l], True).astype(BF16), _block_diag(ssm_c_im[l], True).astype(BF16),
                ssm_d[l].reshape(1, SSM_WIDTH), ssm_w_glu[l].astype(BF16))
        lqk = jnp.stack([diff_lq1[l], diff_lk1[l], diff_lq2[l], diff_lk2[l]])
        subln = diff_subln[l].reshape(1, DIFF_V_DIM)

        qa, ka, va, kab, vab, sg, ub, qc, kc, vc, kcb, vcb = _proj(hp, w_in_bf, cos_p, sin_p)
        sb = _sb_prompt(qa.reshape(bsz, tp, -1), kab.reshape(bsz, tp, -1), vab.reshape(bsz, tp, -1), u2, pad)
        df = _diff_prompt(qc.reshape(bsz, tp, -1), kcb.reshape(bsz, tp, -1), vcb.reshape(bsz, tp, -1),
                          lqk, subln, pad, lam_init)
        ssm, hre, him = _ssm_prompt(ub.reshape(bsz, tp, -1), par, mats, pad)
        hp = _out_ln(sb.reshape(bsz * tp, -1), ssm.reshape(bsz * tp, -1), df.reshape(bsz * tp, -1), sg, hp,
                     w_out_bf, ln_g[l], ln_b[l], alpha)
        for lst, r in zip(rows_p, (ka, va, kc, vc)):
            lst.append(r.reshape(bsz, tp, -1)[:, pad:])
        rows_p[4].append(hre.reshape(bsz, SSM_GROUPS, SSM_STATE))
        rows_p[5].append(him.reshape(bsz, SSM_GROUPS, SSM_STATE))

        qa, ka, va, _, _, sg, ub, qc, kc, vc, _, _ = _proj(hs, w_in_bf, cos_s, sin_s)
        sb, df = _sample_attn(
            l, page_table,
            _per_batch_heads(qa, nb, steps, SB_HEADS, SB_HEAD_DIM),
            _per_batch_heads(qc, nb, steps, 2 * DIFF_HEADS, DIFF_QK_DIM),
            _new_rows(ka, nb, steps), _new_rows(va, nb, steps), _new_rows(kc, nb, steps), _new_rows(vc, nb, steps),
            cka, cva, ckc, cvc, u2, lqk, subln, steps, lam_init)
        sb = sb[:, :steps].reshape(nb * steps, -1)
        df = df[:, :steps].reshape(nb * steps, -1)
        u_tm = ub.reshape(nb, steps, -1).swapaxes(0, 1).reshape(steps * nb, -1)
        ssm_tm, hre, him = _ssm_sample(u_tm, state_ssm_re[l].reshape(nb, SSM_LANES),
                                       state_ssm_im[l].reshape(nb, SSM_LANES), par, mats, nb, steps)
        ssm = ssm_tm.reshape(steps, nb, -1).swapaxes(0, 1).reshape(nb * steps, -1)
        hs = _out_ln(sb, ssm, df, sg, hs, w_out_bf, ln_g[l], ln_b[l], alpha)
        for lst, r in zip(rows_s, (ka, va, kc, vc)):
            lst.append(r.reshape(nb, steps, -1))
        rows_s[4].append(hre.reshape(nb, SSM_GROUPS, SSM_STATE))
        rows_s[5].append(him.reshape(nb, SSM_GROUPS, SSM_STATE))

    def stack(lst, heads):
        a = jnp.stack(lst)
        return a.reshape(a.shape[:3] + (heads, a.shape[3] // heads))

    y_prompt = hp.reshape(bsz, tp, d)[:, pad + N_META:]
    y_sample = hs.reshape(nb, steps, d)
    outs = []
    for rows in (rows_p, rows_s):
        outs += [stack(rows[0], SB_HEADS), stack(rows[1], SB_HEADS), stack(rows[2], 2 * DIFF_HEADS),
                 stack(rows[3], DIFF_HEADS), jnp.stack(rows[4]), jnp.stack(rows[5])]
    return (y_prompt, y_sample, *outs)
```

```python
import functools
import math

import jax
import jax.numpy as jnp
from jax import lax
from jax.experimental import pallas as pl
from jax.experimental.pallas import tpu as pltpu

F32 = jnp.float32
BF16 = jnp.bfloat16

N_META = 16
LN_EPS = 1e-5
ROPE_THETA = 10000.0
NEG_INF = -1e30

SB_HEADS = 4
SB_HEAD_DIM = 64
SB_WIDTH = SB_HEADS * SB_HEAD_DIM
SSM_GROUPS = 16
SSM_GROUP = 16
SSM_STATE = 64
SSM_WIDTH = SSM_GROUPS * SSM_GROUP
SSM_LANES = SSM_GROUPS * SSM_STATE
DIFF_HEADS = 4
DIFF_QK_DIM = 64
DIFF_V_DIM = 128
DIFF_WIDTH = DIFF_HEADS * DIFF_V_DIM
QK_SCALE = 0.125 * math.log2(math.e)

LANES = 128
SUBLANES = 8
TB = 128
TQ = 2 * TB
PAGES_PER_STEP = 8
ROW_TILES = (544, 512, 384, 256, 128)
VMEM_LIMIT = 56 * 1024 * 1024

_NT = (((1,), (1,)), ((), ()))


def _params(sem):
    return pltpu.CompilerParams(dimension_semantics=sem, vmem_limit_bytes=VMEM_LIMIT)


def _dot(a, b):
    return jnp.dot(a, b, preferred_element_type=F32)


def _dot_nt(a, b):
    return lax.dot_general(a, b, _NT, preferred_element_type=F32)


def _sigmoid(x):
    return 1.0 / (1.0 + jnp.exp(-x))


def _layer_norm(x, g, b):
    xc = x - jnp.mean(x, axis=-1, keepdims=True)
    var = jnp.mean(xc * xc, axis=-1, keepdims=True)
    return xc * lax.rsqrt(var + LN_EPS) * g + b


def _row_tile(rows):
    for t in ROW_TILES:
        if rows % t == 0:
            return t
    raise ValueError(f"row count {rows} is not a multiple of {LANES}")


def _ln_kernel(x_ref, g_ref, b_ref, o_ref):
    o_ref[...] = _layer_norm(x_ref[...], g_ref[...], b_ref[...])


def _ln_rows(x, g, b):
    rows, d = x.shape
    tm = _row_tile(rows)
    return pl.pallas_call(
        _ln_kernel,
        grid=(rows // tm,),
        in_specs=[pl.BlockSpec((tm, d), lambda i: (i, 0)),
                  pl.BlockSpec((1, d), lambda i: (0, 0)),
                  pl.BlockSpec((1, d), lambda i: (0, 0))],
        out_specs=pl.BlockSpec((tm, d), lambda i: (i, 0)),
        out_shape=jax.ShapeDtypeStruct((rows, d), F32),
        compiler_params=_params(("parallel",)),
        name="ln_in",
    )(x, g.reshape(1, d), b.reshape(1, d))


def _rope(x, cos, sin_signed):
    lane = lax.broadcasted_iota(jnp.int32, (x.shape[0], LANES), 1)
    first_half = (lane % DIFF_QK_DIM) < (DIFF_QK_DIM // 2)
    outs = []
    for c in range(x.shape[1] // LANES):
        xc = x[:, c * LANES:(c + 1) * LANES]
        partner = jnp.where(first_half,
                            pltpu.roll(xc, LANES - DIFF_QK_DIM // 2, axis=1),
                            pltpu.roll(xc, DIFF_QK_DIM // 2, axis=1))
        outs.append(xc * cos + partner * sin_signed)
    return jnp.concatenate(outs, axis=1)


_C_QA, _C_KA, _C_VA, _C_GA = 0, 256, 512, 768
_C_UB, _C_GB = 1024, 1280
_C_QC, _C_KC, _C_VC, _C_GC, _C_END = 1536, 2048, 2560, 3072, 3584


def _proj_kernel(h_ref, w_ref, cos_ref, sin_ref,
                 qa_ref, ka_ref, va_ref, kab_ref, vab_ref, sg_ref, ub_ref,
                 qc_ref, kc_ref, vc_ref, kcb_ref, vcb_ref):
    hb = h_ref[...].astype(BF16)

    def mm(c0, c1):
        return _dot(hb, w_ref[:, c0:c1])

    def silu(g):
        return g * _sigmoid(g)

    cos = cos_ref[...]
    sin = sin_ref[...]
    qa_ref[...] = (mm(_C_QA, _C_KA) * QK_SCALE).astype(BF16)
    ka = mm(_C_KA, _C_VA)
    ka_ref[...] = ka
    kab_ref[...] = ka.astype(BF16)
    va = mm(_C_VA, _C_GA)
    va_ref[...] = va
    vab_ref[...] = va.astype(BF16)
    sg_ref[:, 0:SB_WIDTH] = silu(mm(_C_GA, _C_UB))
    ub_ref[...] = mm(_C_UB, _C_GB)
    sg_ref[:, SB_WIDTH:SB_WIDTH + SSM_WIDTH] = silu(mm(_C_GB, _C_QC))
    qc_ref[...] = (_rope(mm(_C_QC, _C_KC), cos, sin) * QK_SCALE).astype(BF16)
    kc = _rope(mm(_C_KC, _C_VC), cos, sin)
    kc_ref[...] = kc
    kcb_ref[...] = kc.astype(BF16)
    vc = mm(_C_VC, _C_GC)
    vc_ref[...] = vc
    vcb_ref[...] = vc.astype(BF16)
    sg_ref[:, SB_WIDTH + SSM_WIDTH:] = silu(mm(_C_GC, _C_END))


def _proj(h, w_bf, cos, sin, t_real):
    bsz, tp, d = h.shape
    tm = _row_tile(tp)
    outs = [(SB_WIDTH, BF16, tp), (SB_WIDTH, F32, t_real), (SB_WIDTH, F32, t_real),
            (SB_WIDTH, BF16, tp), (SB_WIDTH, BF16, tp), (d, F32, tp), (SSM_WIDTH, F32, tp),
            (DIFF_WIDTH, BF16, tp), (DIFF_WIDTH, F32, t_real), (DIFF_WIDTH, F32, t_real),
            (DIFF_WIDTH, BF16, tp), (DIFF_WIDTH, BF16, tp)]
    return pl.pallas_call(
        _proj_kernel,
        grid=(bsz, tp // tm),
        in_specs=[pl.BlockSpec((None, tm, d), lambda b, i: (b, i, 0)),
                  pl.BlockSpec(w_bf.shape, lambda b, i: (0, 0)),
                  pl.BlockSpec((tm, LANES), lambda b, i: (i, 0)),
                  pl.BlockSpec((tm, LANES), lambda b, i: (i, 0))],
        out_specs=[pl.BlockSpec((None, tm, w), lambda b, i: (b, i, 0)) for w, _, _ in outs],
        out_shape=[jax.ShapeDtypeStruct((bsz, t, w), dt) for w, dt, t in outs],
        compiler_params=_params(("parallel", "parallel")),
        name="proj",
    )(h, w_bf, cos, sin)


def _out_kernel(sb_ref, ssm_ref, df_ref, sg_ref, h_ref, w_ref, g_ref, b_ref, o_ref, *, alpha):
    mixed = jnp.concatenate([sb_ref[...], ssm_ref[...], df_ref[...]], axis=1) * sg_ref[...]
    out = _dot(mixed.astype(BF16), w_ref[...])
    o_ref[...] = _layer_norm(alpha * h_ref[...] + out, g_ref[...], b_ref[...])


def _out_ln(sb, ssm, df, sg, h, w_bf, g, b, alpha):
    rows, d = h.shape
    tm = _row_tile(rows)

    def rowspec(w):
        return pl.BlockSpec((tm, w), lambda i: (i, 0))

    return pl.pallas_call(
        functools.partial(_out_kernel, alpha=alpha),
        grid=(rows // tm,),
        in_specs=[rowspec(SB_WIDTH), rowspec(SSM_WIDTH), rowspec(DIFF_WIDTH), rowspec(d), rowspec(d),
                  pl.BlockSpec(w_bf.shape, lambda i: (0, 0)),
                  pl.BlockSpec((1, d), lambda i: (0, 0)),
                  pl.BlockSpec((1, d), lambda i: (0, 0))],
        out_specs=rowspec(d),
        out_shape=jax.ShapeDtypeStruct((rows, d), F32),
        compiler_params=_params(("parallel",)),
        name="out_ln",
    )(sb, ssm, df, sg, h, w_bf, g.reshape(1, d), b.reshape(1, d))


def _sb_weights(z, u2, carry, mask):
    drop = jnp.maximum(z, 0.0) + jnp.log2(1.0 + jnp.exp2(-jnp.abs(z)))
    if mask is not None:
        drop = jnp.where(mask, drop, 0.0)
    hi = drop.astype(BF16)
    lo = (drop - hi.astype(F32)).astype(BF16)
    rows, nblk = z.shape[0], z.shape[1] // TB
    blocks = [slice(n * TB, (n + 1) * TB) for n in range(nblk)]
    sums = _dot(jnp.concatenate([jnp.concatenate([hi[:, b], lo[:, b]], axis=1) for b in blocks], axis=0), u2)
    args = [None] * nblk
    for n in reversed(range(nblk)):
        blk = sums[n * rows:(n + 1) * rows]
        args[n] = z[:, blocks[n]] + blk[:, :TB] + carry
        carry = carry + blk[:, TB:]
    w = jnp.exp2(jnp.concatenate(args, axis=1))
    if mask is not None:
        w = jnp.where(mask, w, 0.0)
    return w, carry


def _head_masked(x, heads, width):
    col = lax.broadcasted_iota(jnp.int32, x.shape, 1)
    return [jnp.where((col >= h * width) & (col < (h + 1) * width), x, 0.0).astype(BF16)
            for h in range(heads)]


def _causal_sweep(i, scores, weights, values, diag_mask):
    def then_values(c):
        if values is not None:
            values(c)

    scores(i)

    @pl.when(i == 0)
    def _only_chunk():
        weights(i, diag_mask)

    @pl.when(i >= 1)
    def _head():
        weights(i, diag_mask)
        scores(i - 1)

    def body(t, carry):
        then_values(i - t)
        weights(i - 1 - t, None)
        scores(i - 2 - t)
        return carry
    lax.fori_loop(0, jnp.maximum(i - 1, 0), body, 0)

    @pl.when(i >= 1)
    def _tail():
        then_values(1)
        weights(0, None)

    then_values(0)


def _sb_prompt_kernel(q_ref, k_ref, v_ref, u2_ref, o_ref,
                      vbd_ref, qm_ref, z_ref, w_ref, acc_ref, carry_ref, *, nchunk):
    i = pl.program_id(1)

    @pl.when(i == 0)
    def _build_values():
        def build(c, carry):
            vb = v_ref[0, pl.ds(pl.multiple_of(c * TQ, TQ), TQ), :].astype(F32)
            for h, vm in enumerate(_head_masked(vb, SB_HEADS, SB_HEAD_DIM)):
                vbd_ref[c, h * TQ:(h + 1) * TQ, :] = vm
            return carry
        lax.fori_loop(0, nchunk, build, 0)

    for h, qh in enumerate(_head_masked(q_ref[0].astype(F32), SB_HEADS, SB_HEAD_DIM)):
        qm_ref[h] = qh
    acc_ref[...] = jnp.zeros_like(acc_ref)
    carry_ref[...] = jnp.zeros_like(carry_ref)
    row = lax.broadcasted_iota(jnp.int32, (TQ, TQ), 0)
    col = lax.broadcasted_iota(jnp.int32, (TQ, TQ), 1)

    def scores(c):
        kb = k_ref[0, pl.ds(pl.multiple_of(c * TQ, TQ), TQ), :]
        for h in range(SB_HEADS):
            z_ref[h] = _dot_nt(qm_ref[h], kb)

    def weights(c, mask):
        del c
        for h in range(SB_HEADS):
            w, carry_ref[h] = _sb_weights(z_ref[h], u2_ref[...], carry_ref[h], mask)
            w_ref[:, h * TQ:(h + 1) * TQ] = w.astype(BF16)

    def values(c):
        acc_ref[...] += _dot(w_ref[...], vbd_ref[c])

    _causal_sweep(i, scores, weights, values, col < row)
    o_ref[0] = acc_ref[...]


def _sb_prompt(q, k, v, u2):
    bsz, tp, w = q.shape
    nchunk = tp // TQ
    return pl.pallas_call(
        functools.partial(_sb_prompt_kernel, nchunk=nchunk),
        grid=(bsz, nchunk),
        in_specs=[pl.BlockSpec((1, TQ, w), lambda b, i: (b, i, 0)),
                  pl.BlockSpec((1, tp, w), lambda b, i: (b, 0, 0)),
                  pl.BlockSpec((1, tp, w), lambda b, i: (b, 0, 0)),
                  pl.BlockSpec(u2.shape, lambda b, i: (0, 0))],
        out_specs=pl.BlockSpec((1, TQ, w), lambda b, i: (b, i, 0)),
        out_shape=jax.ShapeDtypeStruct((bsz, tp, w), F32),
        scratch_shapes=[pltpu.VMEM((nchunk, SB_HEADS * TQ, w), BF16),
                        pltpu.VMEM((SB_HEADS, TQ, w), BF16),
                        pltpu.VMEM((SB_HEADS, TQ, TQ), F32),
                        pltpu.VMEM((TQ, SB_HEADS * TQ), BF16),
                        pltpu.VMEM((TQ, w), F32),
                        pltpu.VMEM((SB_HEADS, TQ, TB), F32)],
        compiler_params=_params(("parallel", "arbitrary")),
        name="sb_prompt",
    )(q, k, v, u2)


def _diff_lambda(lqk_ref, lam_init):
    lqk = lqk_ref[...]
    s1 = jnp.sum(lqk[0:1] * lqk[1:2], axis=1, keepdims=True)
    s2 = jnp.sum(lqk[2:3] * lqk[3:4], axis=1, keepdims=True)
    return jnp.exp(s1) - jnp.exp(s2) + lam_init


def _diff_combine(acc0, l0, acc1, l1, lam, subln, lam_init):
    d = acc0 / l0 - lam * (acc1 / l1)
    return d * lax.rsqrt(jnp.mean(d * d, axis=1, keepdims=True) + LN_EPS) * subln * (1.0 - lam_init)


def _diff_prompt_kernel(q_ref, k_ref, v_ref, lqk_ref, subln_ref, o_ref,
                        qs_ref, s_ref, mx_ref, l_ref, acc_ref, *, lam_init):
    i = pl.program_id(1)
    half = 2 * DIFF_QK_DIM * 2
    qf = q_ref[0].astype(F32)
    for hh in range(DIFF_HEADS // 2):
        maps = _head_masked(qf[:, hh * half:(hh + 1) * half], 4, DIFF_QK_DIM)
        for hl in range(2):
            h = hh * 2 + hl
            qs_ref[h, 0:TQ, :] = maps[hl * 2]
            qs_ref[h, TQ:2 * TQ, :] = maps[hl * 2 + 1]
    row = lax.broadcasted_iota(jnp.int32, (2 * TQ, TQ), 0) % TQ
    col = lax.broadcasted_iota(jnp.int32, (2 * TQ, TQ), 1)
    diag_mask = col <= row

    mx_ref[...] = jnp.full(mx_ref.shape, NEG_INF, F32)
    l_ref[...] = jnp.zeros_like(l_ref)
    acc_ref[...] = jnp.zeros_like(acc_ref)

    def scores(c):
        rows = pl.ds(pl.multiple_of(c * TQ, TQ), TQ)
        for h in range(DIFF_HEADS):
            s_ref[h] = _dot_nt(qs_ref[h], k_ref[0, rows, (h // 2) * half:(h // 2 + 1) * half])

    def masked_scores(h, mask):
        s = s_ref[h]
        return s if mask is None else jnp.where(mask, s, NEG_INF)

    def row_max(c, mask):
        del c
        for h in range(DIFF_HEADS):
            s = masked_scores(h, mask)
            mx_ref[h] = jnp.maximum(mx_ref[h], jnp.maximum(s[:, :LANES], s[:, LANES:]))

    _causal_sweep(i, scores, row_max, None, diag_mask)
    for h in range(DIFF_HEADS):
        mx_ref[h] = jnp.broadcast_to(jnp.max(mx_ref[h], axis=1, keepdims=True), (2 * TQ, LANES))

    def weighted_values(c, mask):
        rows = pl.ds(pl.multiple_of(c * TQ, TQ), TQ)
        for h in range(DIFF_HEADS):
            m = mx_ref[h]
            s = masked_scores(h, mask)
            p0 = jnp.exp2(s[:, :LANES] - m)
            p1 = jnp.exp2(s[:, LANES:] - m)
            l_ref[h] += p0 + p1
            acc_ref[h] += _dot(jnp.concatenate([p0, p1], axis=1).astype(BF16),
                               v_ref[0, rows, h * DIFF_V_DIM:(h + 1) * DIFF_V_DIM])

    _causal_sweep(i, scores, weighted_values, None, diag_mask)

    lam = _diff_lambda(lqk_ref, lam_init)
    for h in range(DIFF_HEADS):
        acc = acc_ref[h]
        l = jnp.sum(l_ref[h], axis=1, keepdims=True)
        o_ref[0, :, h * DIFF_V_DIM:(h + 1) * DIFF_V_DIM] = _diff_combine(
            acc[:TQ], l[:TQ], acc[TQ:], l[TQ:], lam, subln_ref[...], lam_init)


def _diff_prompt(q, k, v, lqk, subln, lam_init):
    bsz, tp, w = q.shape
    stacked = jax.ShapeDtypeStruct((DIFF_HEADS, 2 * TQ, LANES), F32)
    return pl.pallas_call(
        functools.partial(_diff_prompt_kernel, lam_init=lam_init),
        grid=(bsz, tp // TQ),
        in_specs=[pl.BlockSpec((1, TQ, w), lambda b, i: (b, i, 0)),
                  pl.BlockSpec((1, tp, w), lambda b, i: (b, 0, 0)),
                  pl.BlockSpec((1, tp, w), lambda b, i: (b, 0, 0)),
                  pl.BlockSpec(lqk.shape, lambda b, i: (0, 0)),
                  pl.BlockSpec(subln.shape, lambda b, i: (0, 0))],
        out_specs=pl.BlockSpec((1, TQ, w), lambda b, i: (b, i, 0)),
        out_shape=jax.ShapeDtypeStruct((bsz, tp, w), F32),
        scratch_shapes=[pltpu.VMEM((DIFF_HEADS, 2 * TQ, 4 * DIFF_QK_DIM), BF16),
                        pltpu.VMEM((DIFF_HEADS, 2 * TQ, TQ), F32),
                        pltpu.VMEM(stacked.shape, F32),
                        pltpu.VMEM(stacked.shape, F32),
                        pltpu.VMEM(stacked.shape, F32)],
        compiler_params=_params(("parallel", "arbitrary")),
        name="diff_prompt",
    )(q, k, v, lqk, subln)


def _ssm_param_kernel(are_ref, aim_ref, ldt_ref, o_ref):
    lam_re, lam_im = are_ref[...], aim_ref[...]
    dt = jnp.exp(ldt_ref[...])
    mag = jnp.exp(lam_re * dt)
    ab_re = mag * jnp.cos(lam_im * dt)
    ab_im = mag * jnp.sin(lam_im * dt)
    den = lam_re * lam_re + lam_im * lam_im
    nr = ab_re - 1.0
    o_ref[0] = ab_re
    o_ref[1] = ab_im
    o_ref[2] = (nr * lam_re + ab_im * lam_im) / den
    o_ref[3] = (ab_im * lam_re - nr * lam_im) / den


def _ssm_params(a_re, a_im, log_dt):
    out = pl.pallas_call(
        _ssm_param_kernel,
        out_shape=jax.ShapeDtypeStruct((4, SSM_GROUPS, SSM_STATE), F32),
        name="ssm_params",
    )(a_re, a_im, log_dt.reshape(SSM_GROUPS, 1))
    return out.reshape(4, SSM_LANES)


def _cmul(ar, ai, br, bi):
    return ar * br - ai * bi, ar * bi + ai * br


def _ssm_inputs(u, par_ref, bre_ref, bim_ref):
    ub = u.astype(BF16)
    return _cmul(par_ref[2:3, :], par_ref[3:4, :], _dot(ub, bre_ref[...]), _dot(ub, bim_ref[...]))


def _ssm_outputs(u, h_re, h_im, cre_ref, cim_ref, d_ref, wglu_ref):
    y = _dot(h_re.astype(BF16), cre_ref[...]) - _dot(h_im.astype(BF16), cim_ref[...]) + d_ref[...] * u
    y = 0.5 * y * (1.0 + jnp.tanh(math.sqrt(2.0 / math.pi) * (y + 0.044715 * (y * y * y))))
    gl = _dot(y.astype(BF16), wglu_ref[...])
    return gl[:, :SSM_WIDTH] * _sigmoid(gl[:, SSM_WIDTH:])


def _ssm_prompt_kernel(u_ref, par_ref, bre_ref, bim_ref, cre_ref, cim_ref, d_ref, wglu_ref,
                       o_ref, hre_ref, him_ref,
                       xr_ref, xi_ref, hr_ref, hi_ref, cst_ref, st_ref, *, last_row, tt):
    t = pl.program_id(1)
    sub = lax.broadcasted_iota(jnp.int32, (SUBLANES, SSM_LANES), 0)

    @pl.when(t == 0)
    def _init():
        st_ref[...] = jnp.zeros_like(st_ref)
        pr, pi = [par_ref[0:1, :]], [par_ref[1:2, :]]
        for _ in range(SUBLANES - 1):
            nr, ni = _cmul(pr[-1], pi[-1], pr[0], pi[0])
            pr.append(nr)
            pi.append(ni)
        for n, k in enumerate((1, 2, 4)):
            cst_ref[2 * n] = jnp.where(sub >= k, pr[k - 1], 0.0)
            cst_ref[2 * n + 1] = jnp.where(sub >= k, pi[k - 1], 0.0)
        car_r = jnp.zeros((SUBLANES, SSM_LANES), F32)
        car_i = jnp.zeros((SUBLANES, SSM_LANES), F32)
        for s in range(SUBLANES):
            car_r = jnp.where(sub == s, pr[s], car_r)
            car_i = jnp.where(sub == s, pi[s], car_i)
        cst_ref[6] = car_r
        cst_ref[7] = car_i

    u = u_ref[0]
    xr_ref[...], xi_ref[...] = _ssm_inputs(u, par_ref, bre_ref, bim_ref)

    def tile(n, carry):
        cr, ci = carry
        r0 = pl.multiple_of(n * SUBLANES, SUBLANES)
        hr = xr_ref[pl.ds(r0, SUBLANES), :]
        hi = xi_ref[pl.ds(r0, SUBLANES), :]
        for m, k in enumerate((1, 2, 4)):
            dr, di = _cmul(cst_ref[2 * m], cst_ref[2 * m + 1],
                           pltpu.roll(hr, k, axis=0), pltpu.roll(hi, k, axis=0))
            hr, hi = hr + dr, hi + di
        dr, di = _cmul(cst_ref[6], cst_ref[7], cr, ci)
        hr, hi = hr + dr, hi + di
        hr_ref[pl.ds(r0, SUBLANES), :] = hr
        hi_ref[pl.ds(r0, SUBLANES), :] = hi
        last = SUBLANES - 1
        return (jnp.broadcast_to(hr[last:last + 1, :], hr.shape),
                jnp.broadcast_to(hi[last:last + 1, :], hi.shape))

    cr, ci = lax.fori_loop(0, tt // SUBLANES, tile, (st_ref[0], st_ref[1]))
    st_ref[0] = cr
    st_ref[1] = ci

    @pl.when(t == last_row // tt)
    def _final_state():
        hre_ref[0] = hr_ref[last_row % tt:last_row % tt + 1, :]
        him_ref[0] = hi_ref[last_row % tt:last_row % tt + 1, :]

    o_ref[0] = _ssm_outputs(u, hr_ref[...], hi_ref[...], cre_ref, cim_ref, d_ref, wglu_ref)


def _ssm_prompt(u, par, mats, t_real):
    bsz, tp, w = u.shape
    tt = TQ

    def const(a):
        return pl.BlockSpec(a.shape, lambda b, t: (0,) * a.ndim)

    state = jax.ShapeDtypeStruct((bsz, 1, SSM_LANES), F32)
    return pl.pallas_call(
        functools.partial(_ssm_prompt_kernel, last_row=t_real - 1, tt=tt),
        grid=(bsz, tp // tt),
        in_specs=[pl.BlockSpec((1, tt, w), lambda b, t: (b, t, 0)), const(par)] + [const(m) for m in mats],
        out_specs=[pl.BlockSpec((1, tt, w), lambda b, t: (b, t, 0)),
                   pl.BlockSpec((1, 1, SSM_LANES), lambda b, t: (b, 0, 0)),
                   pl.BlockSpec((1, 1, SSM_LANES), lambda b, t: (b, 0, 0))],
        out_shape=[jax.ShapeDtypeStruct((bsz, tp, w), F32), state, state],
        scratch_shapes=[pltpu.VMEM((tt, SSM_LANES), F32)] * 4
        + [pltpu.VMEM((8, SUBLANES, SSM_LANES), F32), pltpu.VMEM((2, SUBLANES, SSM_LANES), F32)],
        compiler_params=_params(("parallel", "arbitrary")),
        name="ssm_prompt",
    )(u, par, *mats)


def _ssm_sample_kernel(u_ref, h0r_ref, h0i_ref, par_ref, bre_ref, bim_ref, cre_ref, cim_ref, d_ref, wglu_ref,
                       o_ref, hre_ref, him_ref, hr_ref, hi_ref, *, nb, steps):
    u = u_ref[...]
    x_re, x_im = _ssm_inputs(u, par_ref, bre_ref, bim_ref)
    ab_re, ab_im = par_ref[0:1, :], par_ref[1:2, :]
    hr, hi = h0r_ref[...], h0i_ref[...]
    for s in range(steps):
        dr, di = _cmul(ab_re, ab_im, hr, hi)
        hr = dr + x_re[s * nb:(s + 1) * nb]
        hi = di + x_im[s * nb:(s + 1) * nb]
        hr_ref[s * nb:(s + 1) * nb, :] = hr
        hi_ref[s * nb:(s + 1) * nb, :] = hi
    hre_ref[...] = hr
    him_ref[...] = hi
    o_ref[...] = _ssm_outputs(u, hr_ref[...], hi_ref[...], cre_ref, cim_ref, d_ref, wglu_ref)


def _ssm_sample(u_tm, h0r, h0i, par, mats, nb, steps):
    rows, w = u_tm.shape
    state = jax.ShapeDtypeStruct((nb, SSM_LANES), F32)
    return pl.pallas_call(
        functools.partial(_ssm_sample_kernel, nb=nb, steps=steps),
        out_shape=[jax.ShapeDtypeStruct((rows, w), F32), state, state],
        scratch_shapes=[pltpu.VMEM((rows, SSM_LANES), F32)] * 2,
        compiler_params=pltpu.CompilerParams(vmem_limit_bytes=VMEM_LIMIT),
        name="ssm_sample",
    )(u_tm, h0r, h0i, par, *mats)


def _sample_attn_kernel(pt_ref, qa_ref, qc_ref, nka_ref, nva_ref, nkc_ref, nvc_ref, *rest, steps, lam_init):
    del pt_ref
    g = PAGES_PER_STEP
    pka, pva, pkc, pvc = (rest[n * g:(n + 1) * g] for n in range(4))
    u2_ref, lqk_ref, subln_ref, osb_ref, odf_ref, acc_a, carry_a, m_c, l_c, acc_c = rest[4 * g:]
    p = pl.program_id(1)
    rows_a = SB_HEADS * SUBLANES
    rows_c = 2 * DIFF_HEADS * SUBLANES
    rows_h = 2 * SUBLANES
    qa = qa_ref[0]
    qc = qc_ref[0]
    u2 = u2_ref[...]

    def sb_block(kt, vt, mask, first):
        carry = jnp.zeros((rows_a, TB), F32) if first else carry_a[...]
        w, carry_a[...] = _sb_weights(_dot(qa, kt), u2, carry, mask)
        out = _dot_nt(w.astype(BF16), vt)
        acc_a[...] = out if first else acc_a[...] + out

    def diff_block(kt, v_of_head, mask, first):
        s = _dot(qc, kt)
        if mask is not None:
            s = jnp.where(mask, s, NEG_INF)
        m_prev = jnp.full((rows_c, 1), NEG_INF, F32) if first else m_c[...]
        m_new = jnp.maximum(m_prev, jnp.max(s, axis=1, keepdims=True))
        alpha = jnp.exp2(m_prev - m_new)
        pr = jnp.exp2(s - m_new)
        psum = jnp.sum(pr, axis=1, keepdims=True)
        m_c[...] = m_new
        l_c[...] = psum if first else alpha * l_c[...] + psum
        pr = pr.astype(BF16)
        for h in range(DIFF_HEADS):
            rows = slice(h * rows_h, (h + 1) * rows_h)
            out = _dot(pr[rows], v_of_head(h))
            acc_c[h] = out if first else alpha[rows] * acc_c[h] + out

    def new_rows(ref):
        x = ref[0]
        return jnp.concatenate([x, jnp.zeros((TB - SUBLANES, x.shape[1]), F32)], axis=0)

    @pl.when(p == 0)
    def _new_tokens():
        tok_a = lax.broadcasted_iota(jnp.int32, (rows_a, TB), 0) % SUBLANES
        key_a = lax.broadcasted_iota(jnp.int32, (rows_a, TB), 1)
        sb_block(new_rows(nka_ref).T.astype(BF16), new_rows(nva_ref).T.astype(BF16),
                 (key_a < tok_a) & (key_a < steps), True)
        tok_c = lax.broadcasted_iota(jnp.int32, (rows_c, TB), 0) % SUBLANES
        key_c = lax.broadcasted_iota(jnp.int32, (rows_c, TB), 1)
        nvc = new_rows(nvc_ref).astype(BF16)
        diff_block(new_rows(nkc_ref).T.astype(BF16),
                   lambda h: nvc[:, h * DIFF_V_DIM:(h + 1) * DIFF_V_DIM],
                   (key_c <= tok_c) & (key_c < steps), True)

    def lane_cat(refs):
        return jnp.concatenate([r[0, 0].astype(BF16) for r in refs], axis=1)

    sb_block(lane_cat(pka), lane_cat(pva), None, False)
    diff_block(lane_cat(pkc),
               lambda h: jnp.concatenate(
                   [r[0, 0, pl.ds(h, TB, stride=DIFF_HEADS), :].astype(BF16) for r in pvc], axis=0),
               None, False)

    @pl.when(p == pl.num_programs(1) - 1)
    def _finish():
        acc = acc_a[...]
        col = lax.broadcasted_iota(jnp.int32, (SUBLANES, SB_WIDTH), 1)
        out = jnp.zeros((SUBLANES, SB_WIDTH), F32)
        for h in range(SB_HEADS):
            own = (col >= h * SB_HEAD_DIM) & (col < (h + 1) * SB_HEAD_DIM)
            out = jnp.where(own, acc[h * SUBLANES:(h + 1) * SUBLANES], out)
        osb_ref[0] = out
        lam = _diff_lambda(lqk_ref, lam_init)
        l = l_c[...]
        for h in range(DIFF_HEADS):
            acc = acc_c[h]
            r0 = h * rows_h
            odf_ref[0, :, h * DIFF_V_DIM:(h + 1) * DIFF_V_DIM] = _diff_combine(
                acc[:SUBLANES], l[r0:r0 + SUBLANES], acc[SUBLANES:], l[r0 + SUBLANES:r0 + rows_h],
                lam, subln_ref[...], lam_init)


def _sample_attn(layer, page_table, qa, qc, nka, nva, nkc, nvc, cka, cva, ckc, cvc, u2, lqk, subln,
                 steps, lam_init):
    nb, n_pages = page_table.shape
    rows_a, rows_c = qa.shape[1], qc.shape[1]
    g = PAGES_PER_STEP
    assert n_pages % g == 0

    def per_batch(a):
        return pl.BlockSpec((1,) + a.shape[1:], lambda b, p, pt: (b, 0, 0))

    def paged(a):
        return [pl.BlockSpec((1, 1) + a.shape[2:],
                             lambda b, p, pt, n=n: (layer, pt[b, n_pages - (p + 1) * g + n], 0, 0))
                for n in range(g)]

    def const(a):
        return pl.BlockSpec(a.shape, lambda b, p, pt: (0,) * a.ndim)

    caches = [cka] * g + [cva] * g + [ckc] * g + [cvc] * g
    grid_spec = pltpu.PrefetchScalarGridSpec(
        num_scalar_prefetch=1,
        grid=(nb, n_pages // g),
        in_specs=[per_batch(qa), per_batch(qc), per_batch(nka), per_batch(nva), per_batch(nkc), per_batch(nvc)]
        + paged(cka) + paged(cva) + paged(ckc) + paged(cvc) + [const(u2), const(lqk), const(subln)],
        out_specs=[pl.BlockSpec((1, SUBLANES, SB_WIDTH), lambda b, p, pt: (b, 0, 0)),
                   pl.BlockSpec((1, SUBLANES, DIFF_WIDTH), lambda b, p, pt: (b, 0, 0))],
        scratch_shapes=[pltpu.VMEM((rows_a, SB_WIDTH), F32),
                        pltpu.VMEM((rows_a, TB), F32),
                        pltpu.VMEM((rows_c, 1), F32),
                        pltpu.VMEM((rows_c, 1), F32),
                        pltpu.VMEM((DIFF_HEADS, 2 * SUBLANES, DIFF_V_DIM), F32)])
    return pl.pallas_call(
        functools.partial(_sample_attn_kernel, steps=steps, lam_init=lam_init),
        grid_spec=grid_spec,
        out_shape=[jax.ShapeDtypeStruct((nb, SUBLANES, SB_WIDTH), F32),
                   jax.ShapeDtypeStruct((nb, SUBLANES, DIFF_WIDTH), F32)],
        compiler_params=_params(("parallel", "arbitrary")),
        name="sample_attn",
    )(page_table, qa, qc, nka, nva, nkc, nvc, *caches, u2, lqk, subln)


def _rope_tables(pos):
    half = DIFF_QK_DIM // 2
    inv = ROPE_THETA ** (-2.0 * jnp.arange(half, dtype=F32) / DIFF_QK_DIM)
    ang = pos.astype(F32)[:, None] * inv[None, :]
    cos, sin = jnp.cos(ang), jnp.sin(ang)
    reps = LANES // DIFF_QK_DIM
    return (jnp.tile(jnp.concatenate([cos, cos], axis=1), (1, reps)),
            jnp.tile(jnp.concatenate([-sin, sin], axis=1), (1, reps)))


def _suffix_sum_matrix():
    r = lax.broadcasted_iota(jnp.int32, (2 * TB, 2 * TB), 0) % TB
    c = lax.broadcasted_iota(jnp.int32, (2 * TB, 2 * TB), 1)
    return jnp.where((c >= TB) | (r >= c), -1.0, 0.0).astype(BF16)


def _block_diag(w, transpose):
    if transpose:
        w = jnp.swapaxes(w, 1, 2)
    g, a, b = w.shape
    eye = jnp.eye(g, dtype=w.dtype)
    return (w[:, :, None, :] * eye[:, None, :, None]).reshape(g * a, g * b)


def _per_batch_heads(x, nb, steps, groups, width):
    x = x.reshape(nb, steps, groups * width)
    x = jnp.pad(x, ((0, 0), (0, SUBLANES - steps), (0, 0)))
    col_group = jnp.arange(groups * width) // width
    own = col_group[None, :] == jnp.arange(groups)[:, None]
    out = jnp.where(own[None, :, None, :], x[:, None, :, :], jnp.zeros((), x.dtype))
    return out.reshape(nb, groups * SUBLANES, groups * width)


def _new_rows(x, nb, steps):
    x = x.reshape(nb, steps, x.shape[-1])
    return jnp.pad(x, ((0, 0), (0, SUBLANES - steps), (0, 0)))


def kernel(x_prompt, x_sample, cache_sb_k, cache_sb_v, cache_diff_k, cache_diff_v, state_ssm_re, state_ssm_im,
           page_table, meta_tokens, ln_in_g, ln_in_b, w_in, w_out, ln_g, ln_b, ssm_a_re, ssm_a_im, ssm_log_dt,
           ssm_b_re, ssm_b_im, ssm_c_re, ssm_c_im, ssm_d, ssm_w_glu, diff_lq1, diff_lk1, diff_lq2, diff_lk2,
           diff_subln):
    depth = w_in.shape[0]
    bsz, seq, d = x_prompt.shape
    nb, steps, _ = x_sample.shape
    n_pool, page = cache_sb_k.shape[1], cache_sb_k.shape[2]
    past_len = page_table.shape[1] * page
    assert page == TB and steps <= SUBLANES
    alpha = (2 * depth) ** 0.25
    t_real = seq + N_META
    pad = (-t_real) % TQ
    tp = t_real + pad

    meta = jnp.broadcast_to(meta_tokens.astype(F32)[None], (bsz, N_META, d))
    xp = jnp.concatenate([meta, x_prompt, jnp.zeros((bsz, pad, d), F32)], axis=1).reshape(bsz * tp, d)
    hp = _ln_rows(xp, ln_in_g, ln_in_b)
    hs = _ln_rows(x_sample.reshape(nb * steps, d), ln_in_g, ln_in_b)

    cos_p, sin_p = _rope_tables(jnp.arange(tp, dtype=jnp.int32))
    cos_s, sin_s = (jnp.tile(a, (nb, 1)) for a in _rope_tables(past_len + jnp.arange(steps, dtype=jnp.int32)))
    u2 = _suffix_sum_matrix()

    def feature_major(c):
        return jnp.transpose(c, (0, 1, 3, 4, 2)).reshape(depth, n_pool, -1, page)

    cka, cva, ckc = feature_major(cache_sb_k), feature_major(cache_sb_v), feature_major(cache_diff_k)
    cvc = cache_diff_v.reshape(depth, n_pool, page * DIFF_HEADS, DIFF_V_DIM)

    rows_p = [[] for _ in range(6)]
    rows_s = [[] for _ in range(6)]
    for l in range(depth):
        lam_init = 0.8 - 0.6 * math.exp(-0.3 * l)
        w_in_bf = w_in[l].astype(BF16)
        w_out_bf = w_out[l].astype(BF16)
        par = _ssm_params(ssm_a_re[l], ssm_a_im[l], ssm_log_dt[l])
        mats = (_block_diag(ssm_b_re[l], True).astype(BF16), _block_diag(ssm_b_im[l], True).astype(BF16),
                _block_diag(ssm_c_re[l], True).astype(BF16), _block_diag(ssm_c_im[l], True).astype(BF16),
                ssm_d[l].reshape(1, SSM_WIDTH), ssm_w_glu[l].astype(BF16))
        lqk = jnp.stack([diff_lq1[l], diff_lk1[l], diff_lq2[l], diff_lk2[l]])
        subln = diff_subln[l].reshape(1, DIFF_V_DIM)

        qa, ka, va, kab, vab, sg, ub, qc, kc, vc, kcb, vcb = _proj(
            hp.reshape(bsz, tp, d), w_in_bf, cos_p, sin_p, t_real)
        sb = _sb_prompt(qa, kab, vab, u2)
        df = _diff_prompt(qc, kcb, vcb, lqk, subln, lam_init)
        ssm, hre, him = _ssm_prompt(ub, par, mats, t_real)
        hp = _out_ln(sb.reshape(bsz * tp, -1), ssm.reshape(bsz * tp, -1), df.reshape(bsz * tp, -1),
                     sg.reshape(bsz * tp, -1), hp, w_out_bf, ln_g[l], ln_b[l], alpha)
        for lst, r in zip(rows_p, (ka, va, kc, vc)):
            lst.append(r)
        rows_p[4].append(hre.reshape(bsz, SSM_GROUPS, SSM_STATE))
        rows_p[5].append(him.reshape(bsz, SSM_GROUPS, SSM_STATE))

        qa, ka, va, _, _, sg, ub, qc, kc, vc, _, _ = (
            r[0] for r in _proj(hs[None], w_in_bf, cos_s, sin_s, nb * steps))
        sb, df = _sample_attn(
            l, page_table,
            _per_batch_heads(qa, nb, steps, SB_HEADS, SB_HEAD_DIM),
            _per_batch_heads(qc, nb, steps, 2 * DIFF_HEADS, DIFF_QK_DIM),
            _new_rows(ka, nb, steps), _new_rows(va, nb, steps), _new_rows(kc, nb, steps), _new_rows(vc, nb, steps),
            cka, cva, ckc, cvc, u2, lqk, subln, steps, lam_init)
        sb = sb[:, :steps].reshape(nb * steps, -1)
        df = df[:, :steps].reshape(nb * steps, -1)
        u_tm = ub.reshape(nb, steps, -1).swapaxes(0, 1).reshape(steps * nb, -1)
        ssm_tm, hre, him = _ssm_sample(u_tm, state_ssm_re[l].reshape(nb, SSM_LANES),
                                       state_ssm_im[l].reshape(nb, SSM_LANES), par, mats, nb, steps)
        ssm = ssm_tm.reshape(steps, nb, -1).swapaxes(0, 1).reshape(nb * steps, -1)
        hs = _out_ln(sb, ssm, df, sg, hs, w_out_bf, ln_g[l], ln_b[l], alpha)
        for lst, r in zip(rows_s, (ka, va, kc, vc)):
            lst.append(r.reshape(nb, steps, -1))
        rows_s[4].append(hre.reshape(nb, SSM_GROUPS, SSM_STATE))
        rows_s[5].append(him.reshape(nb, SSM_GROUPS, SSM_STATE))

    def stack(lst, heads):
        a = jnp.stack(lst)
        return a.reshape(a.shape[:3] + (heads, a.shape[3] // heads))

    y_prompt = hp.reshape(bsz, tp, d)[:, N_META:t_real]
    y_sample = hs.reshape(nb, steps, d)
    outs = []
    for rows in (rows_p, rows_s):
        outs += [stack(rows[0], SB_HEADS), stack(rows[1], SB_HEADS), stack(rows[2], 2 * DIFF_HEADS),
                 stack(rows[3], DIFF_HEADS), jnp.stack(rows[4]), jnp.stack(rows[5])]
    return (y_prompt, y_sample, *outs)
```

```python
import functools
import math

import jax
import jax.numpy as jnp
from jax import lax
from jax.experimental import pallas as pl
from jax.experimental.pallas import tpu as pltpu

F32 = jnp.float32
BF16 = jnp.bfloat16

N_META = 16
LN_EPS = 1e-5
ROPE_THETA = 10000.0
NEG_INF = -1e30

SB_HEADS = 4
SB_HEAD_DIM = 64
SB_WIDTH = SB_HEADS * SB_HEAD_DIM
SSM_GROUPS = 16
SSM_GROUP = 16
SSM_STATE = 64
SSM_WIDTH = SSM_GROUPS * SSM_GROUP
SSM_LANES = SSM_GROUPS * SSM_STATE
DIFF_HEADS = 4
DIFF_QK_DIM = 64
DIFF_V_DIM = 128
DIFF_WIDTH = DIFF_HEADS * DIFF_V_DIM
QK_SCALE = 0.125 * math.log2(math.e)

LANES = 128
SUBLANES = 8
TB = 128
TQ = 2 * TB
PAGES_PER_STEP = 16
ROW_TILES = (544, 512, 384, 256, 128)
VMEM_LIMIT = 56 * 1024 * 1024

_NT = (((1,), (1,)), ((), ()))


def _params(sem):
    return pltpu.CompilerParams(dimension_semantics=sem, vmem_limit_bytes=VMEM_LIMIT)


def _dot(a, b):
    return jnp.dot(a, b, preferred_element_type=F32)


def _dot_nt(a, b):
    return lax.dot_general(a, b, _NT, preferred_element_type=F32)


def _sigmoid(x):
    return 1.0 / (1.0 + jnp.exp(-x))


def _layer_norm(x, g, b):
    xc = x - jnp.mean(x, axis=-1, keepdims=True)
    var = jnp.mean(xc * xc, axis=-1, keepdims=True)
    return xc * lax.rsqrt(var + LN_EPS) * g + b


def _row_tile(rows):
    for t in ROW_TILES:
        if rows % t == 0:
            return t
    raise ValueError(f"row count {rows} is not a multiple of {LANES}")


def _ln_kernel(x_ref, g_ref, b_ref, o_ref):
    o_ref[...] = _layer_norm(x_ref[...], g_ref[...], b_ref[...])


def _ln_rows(x, g, b):
    rows, d = x.shape
    tm = _row_tile(rows)
    return pl.pallas_call(
        _ln_kernel,
        grid=(rows // tm,),
        in_specs=[pl.BlockSpec((tm, d), lambda i: (i, 0)),
                  pl.BlockSpec((1, d), lambda i: (0, 0)),
                  pl.BlockSpec((1, d), lambda i: (0, 0))],
        out_specs=pl.BlockSpec((tm, d), lambda i: (i, 0)),
        out_shape=jax.ShapeDtypeStruct((rows, d), F32),
        compiler_params=_params(("parallel",)),
        name="ln_in",
    )(x, g.reshape(1, d), b.reshape(1, d))


def _rope(x, cos, sin_signed):
    lane = lax.broadcasted_iota(jnp.int32, (x.shape[0], LANES), 1)
    first_half = (lane % DIFF_QK_DIM) < (DIFF_QK_DIM // 2)
    outs = []
    for c in range(x.shape[1] // LANES):
        xc = x[:, c * LANES:(c + 1) * LANES]
        partner = jnp.where(first_half,
                            pltpu.roll(xc, LANES - DIFF_QK_DIM // 2, axis=1),
                            pltpu.roll(xc, DIFF_QK_DIM // 2, axis=1))
        outs.append(xc * cos + partner * sin_signed)
    return jnp.concatenate(outs, axis=1)


_C_QA, _C_KA, _C_VA, _C_GA = 0, 256, 512, 768
_C_UB, _C_GB = 1024, 1280
_C_QC, _C_KC, _C_VC, _C_GC, _C_END = 1536, 2048, 2560, 3072, 3584


def _proj_kernel(h_ref, w_ref, cos_ref, sin_ref,
                 qa_ref, ka_ref, va_ref, kab_ref, vab_ref, sg_ref, ub_ref,
                 qc_ref, kc_ref, vc_ref, kcb_ref, vcb_ref):
    hb = h_ref[...].astype(BF16)

    def mm(c0, c1):
        return _dot(hb, w_ref[:, c0:c1])

    def silu(g):
        return g * _sigmoid(g)

    cos = cos_ref[...]
    sin = sin_ref[...]
    qa_ref[...] = (mm(_C_QA, _C_KA) * QK_SCALE).astype(BF16)
    ka = mm(_C_KA, _C_VA)
    ka_ref[...] = ka
    kab_ref[...] = ka.astype(BF16)
    va = mm(_C_VA, _C_GA)
    va_ref[...] = va
    vab_ref[...] = va.astype(BF16)
    sg_ref[:, 0:SB_WIDTH] = silu(mm(_C_GA, _C_UB))
    ub_ref[...] = mm(_C_UB, _C_GB)
    sg_ref[:, SB_WIDTH:SB_WIDTH + SSM_WIDTH] = silu(mm(_C_GB, _C_QC))
    qc_ref[...] = (_rope(mm(_C_QC, _C_KC), cos, sin) * QK_SCALE).astype(BF16)
    kc = _rope(mm(_C_KC, _C_VC), cos, sin)
    kc_ref[...] = kc
    kcb_ref[...] = kc.astype(BF16)
    vc = mm(_C_VC, _C_GC)
    vc_ref[...] = vc
    vcb_ref[...] = vc.astype(BF16)
    sg_ref[:, SB_WIDTH + SSM_WIDTH:] = silu(mm(_C_GC, _C_END))


def _proj(h, w_bf, cos, sin, t_real):
    bsz, tp, d = h.shape
    tm = _row_tile(tp)
    outs = [(SB_WIDTH, BF16, tp), (SB_WIDTH, F32, t_real), (SB_WIDTH, F32, t_real),
            (SB_WIDTH, BF16, tp), (SB_WIDTH, BF16, tp), (d, F32, tp), (SSM_WIDTH, F32, tp),
            (DIFF_WIDTH, BF16, tp), (DIFF_WIDTH, F32, t_real), (DIFF_WIDTH, F32, t_real),
            (DIFF_WIDTH, BF16, tp), (DIFF_WIDTH, BF16, tp)]
    return pl.pallas_call(
        _proj_kernel,
        grid=(bsz, tp // tm),
        in_specs=[pl.BlockSpec((None, tm, d), lambda b, i: (b, i, 0)),
                  pl.BlockSpec(w_bf.shape, lambda b, i: (0, 0)),
                  pl.BlockSpec((tm, LANES), lambda b, i: (i, 0)),
                  pl.BlockSpec((tm, LANES), lambda b, i: (i, 0))],
        out_specs=[pl.BlockSpec((None, tm, w), lambda b, i: (b, i, 0)) for w, _, _ in outs],
        out_shape=[jax.ShapeDtypeStruct((bsz, t, w), dt) for w, dt, t in outs],
        compiler_params=_params(("parallel", "parallel")),
        name="proj",
    )(h, w_bf, cos, sin)


def _out_kernel(sb_ref, ssm_ref, df_ref, sg_ref, h_ref, w_ref, g_ref, b_ref, o_ref, *, alpha):
    mixed = jnp.concatenate([sb_ref[...], ssm_ref[...], df_ref[...]], axis=1) * sg_ref[...]
    out = _dot(mixed.astype(BF16), w_ref[...])
    o_ref[...] = _layer_norm(alpha * h_ref[...] + out, g_ref[...], b_ref[...])


def _out_ln(sb, ssm, df, sg, h, w_bf, g, b, alpha):
    rows, d = h.shape
    tm = _row_tile(rows)

    def rowspec(w):
        return pl.BlockSpec((tm, w), lambda i: (i, 0))

    return pl.pallas_call(
        functools.partial(_out_kernel, alpha=alpha),
        grid=(rows // tm,),
        in_specs=[rowspec(SB_WIDTH), rowspec(SSM_WIDTH), rowspec(DIFF_WIDTH), rowspec(d), rowspec(d),
                  pl.BlockSpec(w_bf.shape, lambda i: (0, 0)),
                  pl.BlockSpec((1, d), lambda i: (0, 0)),
                  pl.BlockSpec((1, d), lambda i: (0, 0))],
        out_specs=rowspec(d),
        out_shape=jax.ShapeDtypeStruct((rows, d), F32),
        compiler_params=_params(("parallel",)),
        name="out_ln",
    )(sb, ssm, df, sg, h, w_bf, g.reshape(1, d), b.reshape(1, d))


def _sb_weights(z, u2, carry, mask):
    drop = jnp.maximum(z, 0.0) + jnp.log2(1.0 + jnp.exp2(-jnp.abs(z)))
    if mask is not None:
        drop = jnp.where(mask, drop, 0.0)
    hi = drop.astype(BF16)
    lo = (drop - hi.astype(F32)).astype(BF16)
    rows, nblk = z.shape[0], z.shape[1] // TB
    blocks = [slice(n * TB, (n + 1) * TB) for n in range(nblk)]
    sums = _dot(jnp.concatenate([jnp.concatenate([hi[:, b], lo[:, b]], axis=1) for b in blocks], axis=0), u2)
    args = [None] * nblk
    for n in reversed(range(nblk)):
        blk = sums[n * rows:(n + 1) * rows]
        args[n] = z[:, blocks[n]] + blk[:, :TB] + carry
        carry = carry + blk[:, TB:]
    w = jnp.exp2(jnp.concatenate(args, axis=1))
    if mask is not None:
        w = jnp.where(mask, w, 0.0)
    return w, carry


def _head_masked(x, heads, width):
    col = lax.broadcasted_iota(jnp.int32, x.shape, 1)
    return [jnp.where((col >= h * width) & (col < (h + 1) * width), x, 0.0).astype(BF16)
            for h in range(heads)]


def _causal_sweep(i, scores, weights, values, diag_mask):
    def then_values(c):
        if values is not None:
            values(c)

    scores(i)

    @pl.when(i == 0)
    def _only_chunk():
        weights(i, diag_mask)

    @pl.when(i >= 1)
    def _head():
        weights(i, diag_mask)
        scores(i - 1)

    def body(t, carry):
        then_values(i - t)
        weights(i - 1 - t, None)
        scores(i - 2 - t)
        return carry
    lax.fori_loop(0, jnp.maximum(i - 1, 0), body, 0)

    @pl.when(i >= 1)
    def _tail():
        then_values(1)
        weights(0, None)

    then_values(0)


def _diff_lambda(lqk_ref, lam_init):
    lqk = lqk_ref[...]
    s1 = jnp.sum(lqk[0:1] * lqk[1:2], axis=1, keepdims=True)
    s2 = jnp.sum(lqk[2:3] * lqk[3:4], axis=1, keepdims=True)
    return jnp.exp(s1) - jnp.exp(s2) + lam_init


def _diff_combine(acc0, l0, acc1, l1, lam, subln, lam_init):
    d = acc0 / l0 - lam * (acc1 / l1)
    return d * lax.rsqrt(jnp.mean(d * d, axis=1, keepdims=True) + LN_EPS) * subln * (1.0 - lam_init)


def _attn_prompt_kernel(qa_ref, ka_ref, va_ref, u2_ref, qc_ref, kc_ref, vc_ref, lqk_ref, subln_ref,
                        osb_ref, odf_ref,
                        vbd_ref, qm_ref, z_ref, w_ref, acc_a, carry_ref,
                        qs_ref, s_ref, mx_ref, l_ref, acc_c, *, nchunk, lam_init):
    i = pl.program_id(1)

    @pl.when(i == 0)
    def _build_values():
        def build(c, carry):
            vb = va_ref[0, pl.ds(pl.multiple_of(c * TQ, TQ), TQ), :].astype(F32)
            for h, vm in enumerate(_head_masked(vb, SB_HEADS, SB_HEAD_DIM)):
                vbd_ref[c, h * TQ:(h + 1) * TQ, :] = vm
            return carry
        lax.fori_loop(0, nchunk, build, 0)

    for h, qh in enumerate(_head_masked(qa_ref[0].astype(F32), SB_HEADS, SB_HEAD_DIM)):
        qm_ref[h] = qh
    half = 2 * DIFF_QK_DIM * 2
    qf = qc_ref[0].astype(F32)
    for hh in range(DIFF_HEADS // 2):
        maps = _head_masked(qf[:, hh * half:(hh + 1) * half], 4, DIFF_QK_DIM)
        for hl in range(2):
            h = hh * 2 + hl
            qs_ref[h, 0:TQ, :] = maps[hl * 2]
            qs_ref[h, TQ:2 * TQ, :] = maps[hl * 2 + 1]
    acc_a[...] = jnp.zeros_like(acc_a)
    carry_ref[...] = jnp.zeros_like(carry_ref)
    mx_ref[...] = jnp.full(mx_ref.shape, NEG_INF, F32)
    l_ref[...] = jnp.zeros_like(l_ref)
    acc_c[...] = jnp.zeros_like(acc_c)

    row = lax.broadcasted_iota(jnp.int32, (2 * TQ, TQ), 0) % TQ
    col = lax.broadcasted_iota(jnp.int32, (2 * TQ, TQ), 1)
    diag_c = col <= row
    diag_a = (col < row)[:TQ]

    def chunk_rows(c):
        return pl.ds(pl.multiple_of(c * TQ, TQ), TQ)

    def scores_c(c):
        for h in range(DIFF_HEADS):
            s_ref[h] = _dot_nt(qs_ref[h], kc_ref[0, chunk_rows(c), (h // 2) * half:(h // 2 + 1) * half])

    def scores_both(c):
        kb = ka_ref[0, chunk_rows(c), :]
        for h in range(SB_HEADS):
            z_ref[h] = _dot_nt(qm_ref[h], kb)
        scores_c(c)

    def masked_scores(h, mask):
        s = s_ref[h]
        return s if mask is None else jnp.where(mask, s, NEG_INF)

    def weights_a_max_c(c, diag):
        del c
        for h in range(SB_HEADS):
            w, carry_ref[h] = _sb_weights(z_ref[h], u2_ref[...], carry_ref[h], diag_a if diag else None)
            w_ref[:, h * TQ:(h + 1) * TQ] = w.astype(BF16)
        for h in range(DIFF_HEADS):
            s = masked_scores(h, diag_c if diag else None)
            mx_ref[h] = jnp.maximum(mx_ref[h], jnp.maximum(s[:, :LANES], s[:, LANES:]))

    def values_a(c):
        acc_a[...] += _dot(w_ref[...], vbd_ref[c])

    _causal_sweep(i, scores_both, weights_a_max_c, values_a, True)
    osb_ref[0] = acc_a[...]
    for h in range(DIFF_HEADS):
        mx_ref[h] = jnp.broadcast_to(jnp.max(mx_ref[h], axis=1, keepdims=True), (2 * TQ, LANES))

    def weighted_values_c(c, diag):
        for h in range(DIFF_HEADS):
            m = mx_ref[h]
            s = masked_scores(h, diag_c if diag else None)
            p0 = jnp.exp2(s[:, :LANES] - m)
            p1 = jnp.exp2(s[:, LANES:] - m)
            l_ref[h] += p0 + p1
            acc_c[h] += _dot(jnp.concatenate([p0, p1], axis=1).astype(BF16),
                             vc_ref[0, chunk_rows(c), h * DIFF_V_DIM:(h + 1) * DIFF_V_DIM])

    _causal_sweep(i, scores_c, weighted_values_c, None, True)

    lam = _diff_lambda(lqk_ref, lam_init)
    for h in range(DIFF_HEADS):
        acc = acc_c[h]
        l = jnp.sum(l_ref[h], axis=1, keepdims=True)
        odf_ref[0, :, h * DIFF_V_DIM:(h + 1) * DIFF_V_DIM] = _diff_combine(
            acc[:TQ], l[:TQ], acc[TQ:], l[TQ:], lam, subln_ref[...], lam_init)


def _attn_prompt(qa, ka, va, u2, qc, kc, vc, lqk, subln, lam_init):
    bsz, tp, wa = qa.shape
    wc = qc.shape[2]
    nchunk = tp // TQ
    stacked = (DIFF_HEADS, 2 * TQ, LANES)

    def block(w):
        return pl.BlockSpec((1, TQ, w), lambda b, i: (b, i, 0))

    def whole(w):
        return pl.BlockSpec((1, tp, w), lambda b, i: (b, 0, 0))

    def const(a):
        return pl.BlockSpec(a.shape, lambda b, i: (0, 0))

    return pl.pallas_call(
        functools.partial(_attn_prompt_kernel, nchunk=nchunk, lam_init=lam_init),
        grid=(bsz, nchunk),
        in_specs=[block(wa), whole(wa), whole(wa), const(u2), block(wc), whole(wc), whole(wc),
                  const(lqk), const(subln)],
        out_specs=[block(wa), block(wc)],
        out_shape=[jax.ShapeDtypeStruct((bsz, tp, wa), F32), jax.ShapeDtypeStruct((bsz, tp, wc), F32)],
        scratch_shapes=[pltpu.VMEM((nchunk, SB_HEADS * TQ, wa), BF16),
                        pltpu.VMEM((SB_HEADS, TQ, wa), BF16),
                        pltpu.VMEM((SB_HEADS, TQ, TQ), F32),
                        pltpu.VMEM((TQ, SB_HEADS * TQ), BF16),
                        pltpu.VMEM((TQ, wa), F32),
                        pltpu.VMEM((SB_HEADS, TQ, TB), F32),
                        pltpu.VMEM((DIFF_HEADS, 2 * TQ, 4 * DIFF_QK_DIM), BF16),
                        pltpu.VMEM((DIFF_HEADS, 2 * TQ, TQ), F32),
                        pltpu.VMEM(stacked, F32),
                        pltpu.VMEM(stacked, F32),
                        pltpu.VMEM(stacked, F32)],
        compiler_params=_params(("parallel", "arbitrary")),
        name="attn_prompt",
    )(qa, ka, va, u2, qc, kc, vc, lqk, subln)


def _ssm_param_kernel(are_ref, aim_ref, ldt_ref, o_ref):
    lam_re, lam_im = are_ref[...], aim_ref[...]
    dt = jnp.exp(ldt_ref[...])
    mag = jnp.exp(lam_re * dt)
    ab_re = mag * jnp.cos(lam_im * dt)
    ab_im = mag * jnp.sin(lam_im * dt)
    den = lam_re * lam_re + lam_im * lam_im
    nr = ab_re - 1.0
    o_ref[0] = ab_re
    o_ref[1] = ab_im
    o_ref[2] = (nr * lam_re + ab_im * lam_im) / den
    o_ref[3] = (ab_im * lam_re - nr * lam_im) / den


def _ssm_params(a_re, a_im, log_dt):
    out = pl.pallas_call(
        _ssm_param_kernel,
        out_shape=jax.ShapeDtypeStruct((4, SSM_GROUPS, SSM_STATE), F32),
        name="ssm_params",
    )(a_re, a_im, log_dt.reshape(SSM_GROUPS, 1))
    return out.reshape(4, SSM_LANES)


def _cmul(ar, ai, br, bi):
    return ar * br - ai * bi, ar * bi + ai * br


def _ssm_inputs(u, par_ref, bre_ref, bim_ref):
    ub = u.astype(BF16)
    return _cmul(par_ref[2:3, :], par_ref[3:4, :], _dot(ub, bre_ref[...]), _dot(ub, bim_ref[...]))


def _ssm_outputs(u, h_re, h_im, cre_ref, cim_ref, d_ref, wglu_ref):
    y = _dot(h_re.astype(BF16), cre_ref[...]) - _dot(h_im.astype(BF16), cim_ref[...]) + d_ref[...] * u
    y = 0.5 * y * (1.0 + jnp.tanh(math.sqrt(2.0 / math.pi) * (y + 0.044715 * (y * y * y))))
    gl = _dot(y.astype(BF16), wglu_ref[...])
    return gl[:, :SSM_WIDTH] * _sigmoid(gl[:, SSM_WIDTH:])


def _ssm_prompt_kernel(u_ref, par_ref, bre_ref, bim_ref, cre_ref, cim_ref, d_ref, wglu_ref,
                       o_ref, hre_ref, him_ref,
                       xr_ref, xi_ref, hr_ref, hi_ref, cst_ref, st_ref, *, last_row, tt):
    t = pl.program_id(1)
    sub = lax.broadcasted_iota(jnp.int32, (SUBLANES, SSM_LANES), 0)

    @pl.when(t == 0)
    def _init():
        st_ref[...] = jnp.zeros_like(st_ref)
        pr, pi = [par_ref[0:1, :]], [par_ref[1:2, :]]
        for _ in range(SUBLANES - 1):
            nr, ni = _cmul(pr[-1], pi[-1], pr[0], pi[0])
            pr.append(nr)
            pi.append(ni)
        for n, k in enumerate((1, 2, 4)):
            cst_ref[2 * n] = jnp.where(sub >= k, pr[k - 1], 0.0)
            cst_ref[2 * n + 1] = jnp.where(sub >= k, pi[k - 1], 0.0)
        car_r = jnp.zeros((SUBLANES, SSM_LANES), F32)
        car_i = jnp.zeros((SUBLANES, SSM_LANES), F32)
        for s in range(SUBLANES):
            car_r = jnp.where(sub == s, pr[s], car_r)
            car_i = jnp.where(sub == s, pi[s], car_i)
        cst_ref[6] = car_r
        cst_ref[7] = car_i

    u = u_ref[0]
    xr_ref[...], xi_ref[...] = _ssm_inputs(u, par_ref, bre_ref, bim_ref)

    def tile(n, carry):
        cr, ci = carry
        r0 = pl.multiple_of(n * SUBLANES, SUBLANES)
        hr = xr_ref[pl.ds(r0, SUBLANES), :]
        hi = xi_ref[pl.ds(r0, SUBLANES), :]
        for m, k in enumerate((1, 2, 4)):
            dr, di = _cmul(cst_ref[2 * m], cst_ref[2 * m + 1],
                           pltpu.roll(hr, k, axis=0), pltpu.roll(hi, k, axis=0))
            hr, hi = hr + dr, hi + di
        dr, di = _cmul(cst_ref[6], cst_ref[7], cr, ci)
        hr, hi = hr + dr, hi + di
        hr_ref[pl.ds(r0, SUBLANES), :] = hr
        hi_ref[pl.ds(r0, SUBLANES), :] = hi
        last = SUBLANES - 1
        return (jnp.broadcast_to(hr[last:last + 1, :], hr.shape),
                jnp.broadcast_to(hi[last:last + 1, :], hi.shape))

    cr, ci = lax.fori_loop(0, tt // SUBLANES, tile, (st_ref[0], st_ref[1]))
    st_ref[0] = cr
    st_ref[1] = ci

    @pl.when(t == last_row // tt)
    def _final_state():
        hre_ref[0] = hr_ref[last_row % tt:last_row % tt + 1, :]
        him_ref[0] = hi_ref[last_row % tt:last_row % tt + 1, :]

    o_ref[0] = _ssm_outputs(u, hr_ref[...], hi_ref[...], cre_ref, cim_ref, d_ref, wglu_ref)


def _ssm_prompt(u, par, mats, t_real):
    bsz, tp, w = u.shape
    tt = TQ

    def const(a):
        return pl.BlockSpec(a.shape, lambda b, t: (0,) * a.ndim)

    state = jax.ShapeDtypeStruct((bsz, 1, SSM_LANES), F32)
    return pl.pallas_call(
        functools.partial(_ssm_prompt_kernel, last_row=t_real - 1, tt=tt),
        grid=(bsz, tp // tt),
        in_specs=[pl.BlockSpec((1, tt, w), lambda b, t: (b, t, 0)), const(par)] + [const(m) for m in mats],
        out_specs=[pl.BlockSpec((1, tt, w), lambda b, t: (b, t, 0)),
                   pl.BlockSpec((1, 1, SSM_LANES), lambda b, t: (b, 0, 0)),
                   pl.BlockSpec((1, 1, SSM_LANES), lambda b, t: (b, 0, 0))],
        out_shape=[jax.ShapeDtypeStruct((bsz, tp, w), F32), state, state],
        scratch_shapes=[pltpu.VMEM((tt, SSM_LANES), F32)] * 4
        + [pltpu.VMEM((8, SUBLANES, SSM_LANES), F32), pltpu.VMEM((2, SUBLANES, SSM_LANES), F32)],
        compiler_params=_params(("parallel", "arbitrary")),
        name="ssm_prompt",
    )(u, par, *mats)


def _ssm_sample_kernel(u_ref, h0r_ref, h0i_ref, par_ref, bre_ref, bim_ref, cre_ref, cim_ref, d_ref, wglu_ref,
                       o_ref, hre_ref, him_ref, hr_ref, hi_ref, *, nb, steps):
    u = u_ref[...]
    x_re, x_im = _ssm_inputs(u, par_ref, bre_ref, bim_ref)
    ab_re, ab_im = par_ref[0:1, :], par_ref[1:2, :]
    hr, hi = h0r_ref[...], h0i_ref[...]
    for s in range(steps):
        dr, di = _cmul(ab_re, ab_im, hr, hi)
        hr = dr + x_re[s * nb:(s + 1) * nb]
        hi = di + x_im[s * nb:(s + 1) * nb]
        hr_ref[s * nb:(s + 1) * nb, :] = hr
        hi_ref[s * nb:(s + 1) * nb, :] = hi
    hre_ref[...] = hr
    him_ref[...] = hi
    o_ref[...] = _ssm_outputs(u, hr_ref[...], hi_ref[...], cre_ref, cim_ref, d_ref, wglu_ref)


def _ssm_sample(u_tm, h0r, h0i, par, mats, nb, steps):
    rows, w = u_tm.shape
    state = jax.ShapeDtypeStruct((nb, SSM_LANES), F32)
    return pl.pallas_call(
        functools.partial(_ssm_sample_kernel, nb=nb, steps=steps),
        out_shape=[jax.ShapeDtypeStruct((rows, w), F32), state, state],
        scratch_shapes=[pltpu.VMEM((rows, SSM_LANES), F32)] * 2,
        compiler_params=pltpu.CompilerParams(vmem_limit_bytes=VMEM_LIMIT),
        name="ssm_sample",
    )(u_tm, h0r, h0i, par, *mats)


def _sample_attn_kernel(pt_ref, qa_ref, qc_ref, nka_ref, nva_ref, nkc_ref, nvc_ref, *rest, steps, lam_init):
    del pt_ref
    g = PAGES_PER_STEP
    pka, pva, pkc, pvc = (rest[n * g:(n + 1) * g] for n in range(4))
    u2_ref, lqk_ref, subln_ref, osb_ref, odf_ref, acc_a, carry_a, m_c, l_c, acc_c = rest[4 * g:]
    p = pl.program_id(1)
    rows_a = SB_HEADS * SUBLANES
    rows_c = 2 * DIFF_HEADS * SUBLANES
    rows_h = 2 * SUBLANES
    qa = qa_ref[0]
    qc = qc_ref[0]
    u2 = u2_ref[...]

    def sb_block(kt, vt, mask, first):
        carry = jnp.zeros((rows_a, TB), F32) if first else carry_a[...]
        w, carry_a[...] = _sb_weights(_dot(qa, kt), u2, carry, mask)
        out = _dot_nt(w.astype(BF16), vt)
        acc_a[...] = out if first else acc_a[...] + out

    def diff_block(kt, v_of_head, mask, first):
        s = _dot(qc, kt)
        if mask is not None:
            s = jnp.where(mask, s, NEG_INF)
        m_prev = jnp.full((rows_c, 1), NEG_INF, F32) if first else m_c[...]
        m_new = jnp.maximum(m_prev, jnp.max(s, axis=1, keepdims=True))
        alpha = jnp.exp2(m_prev - m_new)
        pr = jnp.exp2(s - m_new)
        psum = jnp.sum(pr, axis=1, keepdims=True)
        m_c[...] = m_new
        l_c[...] = psum if first else alpha * l_c[...] + psum
        pr = pr.astype(BF16)
        for h in range(DIFF_HEADS):
            rows = slice(h * rows_h, (h + 1) * rows_h)
            out = _dot(pr[rows], v_of_head(h))
            acc_c[h] = out if first else alpha[rows] * acc_c[h] + out

    def new_rows(ref):
        x = ref[0]
        return jnp.concatenate([x, jnp.zeros((TB - SUBLANES, x.shape[1]), F32)], axis=0)

    @pl.when(p == 0)
    def _new_tokens():
        tok_a = lax.broadcasted_iota(jnp.int32, (rows_a, TB), 0) % SUBLANES
        key_a = lax.broadcasted_iota(jnp.int32, (rows_a, TB), 1)
        sb_block(new_rows(nka_ref).T.astype(BF16), new_rows(nva_ref).T.astype(BF16),
                 (key_a < tok_a) & (key_a < steps), True)
        tok_c = lax.broadcasted_iota(jnp.int32, (rows_c, TB), 0) % SUBLANES
        key_c = lax.broadcasted_iota(jnp.int32, (rows_c, TB), 1)
        nvc = new_rows(nvc_ref).astype(BF16)
        diff_block(new_rows(nkc_ref).T.astype(BF16),
                   lambda h: nvc[:, h * DIFF_V_DIM:(h + 1) * DIFF_V_DIM],
                   (key_c <= tok_c) & (key_c < steps), True)

    def lane_cat(refs):
        return jnp.concatenate([r[0, 0].astype(BF16) for r in refs], axis=1)

    sb_block(lane_cat(pka), lane_cat(pva), None, False)
    diff_block(lane_cat(pkc),
               lambda h: jnp.concatenate(
                   [r[0, 0, pl.ds(h, TB, stride=DIFF_HEADS), :].astype(BF16) for r in pvc], axis=0),
               None, False)

    @pl.when(p == pl.num_programs(1) - 1)
    def _finish():
        acc = acc_a[...]
        col = lax.broadcasted_iota(jnp.int32, (SUBLANES, SB_WIDTH), 1)
        out = jnp.zeros((SUBLANES, SB_WIDTH), F32)
        for h in range(SB_HEADS):
            own = (col >= h * SB_HEAD_DIM) & (col < (h + 1) * SB_HEAD_DIM)
            out = jnp.where(own, acc[h * SUBLANES:(h + 1) * SUBLANES], out)
        osb_ref[0] = out
        lam = _diff_lambda(lqk_ref, lam_init)
        l = l_c[...]
        for h in range(DIFF_HEADS):
            acc = acc_c[h]
            r0 = h * rows_h
            odf_ref[0, :, h * DIFF_V_DIM:(h + 1) * DIFF_V_DIM] = _diff_combine(
                acc[:SUBLANES], l[r0:r0 + SUBLANES], acc[SUBLANES:], l[r0 + SUBLANES:r0 + rows_h],
                lam, subln_ref[...], lam_init)


def _sample_attn(layer, page_table, qa, qc, nka, nva, nkc, nvc, cka, cva, ckc, cvc, u2, lqk, subln,
                 steps, lam_init):
    nb, n_pages = page_table.shape
    rows_a, rows_c = qa.shape[1], qc.shape[1]
    g = PAGES_PER_STEP
    assert n_pages % g == 0

    def per_batch(a):
        return pl.BlockSpec((1,) + a.shape[1:], lambda b, p, pt: (b, 0, 0))

    def paged(a):
        return [pl.BlockSpec((1, 1) + a.shape[2:],
                             lambda b, p, pt, n=n: (layer, pt[b, n_pages - (p + 1) * g + n], 0, 0))
                for n in range(g)]

    def const(a):
        return pl.BlockSpec(a.shape, lambda b, p, pt: (0,) * a.ndim)

    caches = [cka] * g + [cva] * g + [ckc] * g + [cvc] * g
    grid_spec = pltpu.PrefetchScalarGridSpec(
        num_scalar_prefetch=1,
        grid=(nb, n_pages // g),
        in_specs=[per_batch(qa), per_batch(qc), per_batch(nka), per_batch(nva), per_batch(nkc), per_batch(nvc)]
        + paged(cka) + paged(cva) + paged(ckc) + paged(cvc) + [const(u2), const(lqk), const(subln)],
        out_specs=[pl.BlockSpec((1, SUBLANES, SB_WIDTH), lambda b, p, pt: (b, 0, 0)),
                   pl.BlockSpec((1, SUBLANES, DIFF_WIDTH), lambda b, p, pt: (b, 0, 0))],
        scratch_shapes=[pltpu.VMEM((rows_a, SB_WIDTH), F32),
                        pltpu.VMEM((rows_a, TB), F32),
                        pltpu.VMEM((rows_c, 1), F32),
                        pltpu.VMEM((rows_c, 1), F32),
                        pltpu.VMEM((DIFF_HEADS, 2 * SUBLANES, DIFF_V_DIM), F32)])
    return pl.pallas_call(
        functools.partial(_sample_attn_kernel, steps=steps, lam_init=lam_init),
        grid_spec=grid_spec,
        out_shape=[jax.ShapeDtypeStruct((nb, SUBLANES, SB_WIDTH), F32),
                   jax.ShapeDtypeStruct((nb, SUBLANES, DIFF_WIDTH), F32)],
        compiler_params=_params(("parallel", "arbitrary")),
        name="sample_attn",
    )(page_table, qa, qc, nka, nva, nkc, nvc, *caches, u2, lqk, subln)


def _rope_tables(pos):
    half = DIFF_QK_DIM // 2
    inv = ROPE_THETA ** (-2.0 * jnp.arange(half, dtype=F32) / DIFF_QK_DIM)
    ang = pos.astype(F32)[:, None] * inv[None, :]
    cos, sin = jnp.cos(ang), jnp.sin(ang)
    reps = LANES // DIFF_QK_DIM
    return (jnp.tile(jnp.concatenate([cos, cos], axis=1), (1, reps)),
            jnp.tile(jnp.concatenate([-sin, sin], axis=1), (1, reps)))


def _suffix_sum_matrix():
    r = lax.broadcasted_iota(jnp.int32, (2 * TB, 2 * TB), 0) % TB
    c = lax.broadcasted_iota(jnp.int32, (2 * TB, 2 * TB), 1)
    return jnp.where((c >= TB) | (r >= c), -1.0, 0.0).astype(BF16)


def _block_diag(w, transpose):
    if transpose:
        w = jnp.swapaxes(w, 1, 2)
    g, a, b = w.shape
    eye = jnp.eye(g, dtype=w.dtype)
    return (w[:, :, None, :] * eye[:, None, :, None]).reshape(g * a, g * b)


def _per_batch_heads(x, nb, steps, groups, width):
    x = x.reshape(nb, steps, groups * width)
    x = jnp.pad(x, ((0, 0), (0, SUBLANES - steps), (0, 0)))
    col_group = jnp.arange(groups * width) // width
    own = col_group[None, :] == jnp.arange(groups)[:, None]
    out = jnp.where(own[None, :, None, :], x[:, None, :, :], jnp.zeros((), x.dtype))
    return out.reshape(nb, groups * SUBLANES, groups * width)


def _new_rows(x, nb, steps):
    x = x.reshape(nb, steps, x.shape[-1])
    return jnp.pad(x, ((0, 0), (0, SUBLANES - steps), (0, 0)))


def kernel(x_prompt, x_sample, cache_sb_k, cache_sb_v, cache_diff_k, cache_diff_v, state_ssm_re, state_ssm_im,
           page_table, meta_tokens, ln_in_g, ln_in_b, w_in, w_out, ln_g, ln_b, ssm_a_re, ssm_a_im, ssm_log_dt,
           ssm_b_re, ssm_b_im, ssm_c_re, ssm_c_im, ssm_d, ssm_w_glu, diff_lq1, diff_lk1, diff_lq2, diff_lk2,
           diff_subln):
    depth = w_in.shape[0]
    bsz, seq, d = x_prompt.shape
    nb, steps, _ = x_sample.shape
    n_pool, page = cache_sb_k.shape[1], cache_sb_k.shape[2]
    past_len = page_table.shape[1] * page
    assert page == TB and steps <= SUBLANES
    alpha = (2 * depth) ** 0.25
    t_real = seq + N_META
    pad = (-t_real) % TQ
    tp = t_real + pad

    meta = jnp.broadcast_to(meta_tokens.astype(F32)[None], (bsz, N_META, d))
    xp = jnp.concatenate([meta, x_prompt, jnp.zeros((bsz, pad, d), F32)], axis=1).reshape(bsz * tp, d)
    hp = _ln_rows(xp, ln_in_g, ln_in_b)
    hs = _ln_rows(x_sample.reshape(nb * steps, d), ln_in_g, ln_in_b)

    cos_p, sin_p = _rope_tables(jnp.arange(tp, dtype=jnp.int32))
    cos_s, sin_s = (jnp.tile(a, (nb, 1)) for a in _rope_tables(past_len + jnp.arange(steps, dtype=jnp.int32)))
    u2 = _suffix_sum_matrix()

    def feature_major(c):
        return jnp.transpose(c, (0, 1, 3, 4, 2)).reshape(depth, n_pool, -1, page)

    cka, cva, ckc = feature_major(cache_sb_k), feature_major(cache_sb_v), feature_major(cache_diff_k)
    cvc = cache_diff_v.reshape(depth, n_pool, page * DIFF_HEADS, DIFF_V_DIM)

    rows_p = [[] for _ in range(6)]
    rows_s = [[] for _ in range(6)]
    for l in range(depth):
        lam_init = 0.8 - 0.6 * math.exp(-0.3 * l)
        w_in_bf = w_in[l].astype(BF16)
        w_out_bf = w_out[l].astype(BF16)
        par = _ssm_params(ssm_a_re[l], ssm_a_im[l], ssm_log_dt[l])
        mats = (_block_diag(ssm_b_re[l], True).astype(BF16), _block_diag(ssm_b_im[l], True).astype(BF16),
                _block_diag(ssm_c_re[l], True).astype(BF16), _block_diag(ssm_c_im[l], True).astype(BF16),
                ssm_d[l].reshape(1, SSM_WIDTH), ssm_w_glu[l].astype(BF16))
        lqk = jnp.stack([diff_lq1[l], diff_lk1[l], diff_lq2[l], diff_lk2[l]])
        subln = diff_subln[l].reshape(1, DIFF_V_DIM)

        qa, ka, va, kab, vab, sg, ub, qc, kc, vc, kcb, vcb = _proj(
            hp.reshape(bsz, tp, d), w_in_bf, cos_p, sin_p, t_real)
        sb, df = _attn_prompt(qa, kab, vab, u2, qc, kcb, vcb, lqk, subln, lam_init)
        ssm, hre, him = _ssm_prompt(ub, par, mats, t_real)
        hp = _out_ln(sb.reshape(bsz * tp, -1), ssm.reshape(bsz * tp, -1), df.reshape(bsz * tp, -1),
                     sg.reshape(bsz * tp, -1), hp, w_out_bf, ln_g[l], ln_b[l], alpha)
        for lst, r in zip(rows_p, (ka, va, kc, vc)):
            lst.append(r)
        rows_p[4].append(hre.reshape(bsz, SSM_GROUPS, SSM_STATE))
        rows_p[5].append(him.reshape(bsz, SSM_GROUPS, SSM_STATE))

        qa, ka, va, _, _, sg, ub, qc, kc, vc, _, _ = (
            r[0] for r in _proj(hs[None], w_in_bf, cos_s, sin_s, nb * steps))
        sb, df = _sample_attn(
            l, page_table,
            _per_batch_heads(qa, nb, steps, SB_HEADS, SB_HEAD_DIM),
            _per_batch_heads(qc, nb, steps, 2 * DIFF_HEADS, DIFF_QK_DIM),
            _new_rows(ka, nb, steps), _new_rows(va, nb, steps), _new_rows(kc, nb, steps), _new_rows(vc, nb, steps),
            cka, cva, ckc, cvc, u2, lqk, subln, steps, lam_init)
        sb = sb[:, :steps].reshape(nb * steps, -1)
        df = df[:, :steps].reshape(nb * steps, -1)
        u_tm = ub.reshape(nb, steps, -1).swapaxes(0, 1).reshape(steps * nb, -1)
        ssm_tm, hre, him = _ssm_sample(u_tm, state_ssm_re[l].reshape(nb, SSM_LANES),
                                       state_ssm_im[l].reshape(nb, SSM_LANES), par, mats, nb, steps)
        ssm = ssm_tm.reshape(steps, nb, -1).swapaxes(0, 1).reshape(nb * steps, -1)
        hs = _out_ln(sb, ssm, df, sg, hs, w_out_bf, ln_g[l], ln_b[l], alpha)
        for lst, r in zip(rows_s, (ka, va, kc, vc)):
            lst.append(r.reshape(nb, steps, -1))
        rows_s[4].append(hre.reshape(nb, SSM_GROUPS, SSM_STATE))
        rows_s[5].append(him.reshape(nb, SSM_GROUPS, SSM_STATE))

    def stack(lst, heads):
        a = jnp.stack(lst)
        return a.reshape(a.shape[:3] + (heads, a.shape[3] // heads))

    y_prompt = hp.reshape(bsz, tp, d)[:, N_META:t_real]
    y_sample = hs.reshape(nb, steps, d)
    outs = []
    for rows in (rows_p, rows_s):
        outs += [stack(rows[0], SB_HEADS), stack(rows[1], SB_HEADS), stack(rows[2], 2 * DIFF_HEADS),
                 stack(rows[3], DIFF_HEADS), jnp.stack(rows[4]), jnp.stack(rows[5])]
    return (y_prompt, y_sample, *outs)
```
